```python
import math
import jax, jax.numpy as jnp
from jax import lax
import numpy as np

D_MODEL = 2048
BATCH = 2
SEQ = 16384
DEPTH = 2

CHUNK = 64
D_CONV = 1024
CONV_K = 3
N_HEADS = 8
HEAD_DIM = 128
D_ATTN = N_HEADS * HEAD_DIM
Q_BLOCK = 128
N_BRANCH = 2
P_TOTAL = 3 * D_CONV + 3 * D_ATTN + N_HEADS + N_BRANCH * D_MODEL
D_FF = 5632
N_EXPERTS = 8
TOP_K = 2
D_FF_EXPERT = 7168
MOE_GROUP = 256
N_DENSE = (DEPTH + 1) // 2
N_MOE = DEPTH // 2
RMS_EPS = 1e-6

kernel_name = "hybrid_conv_forgetting_attn_moe_trunk"


def rmsnorm(x, g):
    xf = x.astype(jnp.float32)
    y = xf * lax.rsqrt(jnp.mean(xf * xf, axis=-1, keepdims=True) + RMS_EPS)
    return (y * g.astype(jnp.float32)).astype(x.dtype)


def causal_dwconv(u, w):
    S = u.shape[1]
    up = jnp.pad(u, ((0, 0), (CONV_K - 1, 0), (0, 0)))
    y = up[:, 0:S] * w[0]
    for j in range(1, CONV_K):
        y = y + up[:, j:j + S] * w[j]
    return y


def forgetting_attention(q, k, v, log_f):
    Bsz, S, H, Dh = q.shape
    nb = S // Q_BLOCK
    scale = 1.0 / math.sqrt(Dh)
    c = jnp.cumsum(log_f, axis=1)
    qb = q.reshape(Bsz, nb, Q_BLOCK, H, Dh).transpose(1, 0, 3, 2, 4)
    cq = c.reshape(Bsz, nb, Q_BLOCK, H).transpose(1, 0, 3, 2)
    kT = k.transpose(0, 2, 1, 3)
    vT = v.transpose(0, 2, 1, 3)
    ck = c.transpose(0, 2, 1)
    key_pos = jnp.arange(S)

    def block(args):
        qi, cqi, bi = args
        s = jnp.einsum('bhqd,bhkd->bhqk', qi, kT, preferred_element_type=jnp.float32) * scale
        s = s + cqi[..., None] - ck[:, :, None, :]
        q_pos = bi * Q_BLOCK + jnp.arange(Q_BLOCK)
        mask = key_pos[None, :] <= q_pos[:, None]
        s = jnp.where(mask[None, None], s, -jnp.inf)
        p = jax.nn.softmax(s, axis=-1)
        return jnp.einsum('bhqk,bhkd->bhqd', p.astype(vT.dtype), vT)

    o = lax.map(block, (qb, cq, jnp.arange(nb)))
    return o.transpose(1, 0, 3, 2, 4).reshape(Bsz, S, H * Dh)


def hybrid_mixer(a, w_in, b_forget, conv_w, w_conv_out, w_attn_out, w_o):
    Bsz, S, _ = a.shape
    proj = a @ w_in
    sizes = [D_CONV, D_CONV, D_CONV, D_ATTN, D_ATTN, D_ATTN, N_HEADS]
    offs = [int(o) for o in np.cumsum(sizes)]
    b_gate, c_gate, u, q, k, v, f_logit, gates = jnp.split(proj, offs, axis=-1)
    y_conv = (b_gate * causal_dwconv(c_gate * u, conv_w)) @ w_conv_out
    log_f = jax.nn.log_sigmoid(f_logit.astype(jnp.float32) + b_forget.astype(jnp.float32))
    o = forgetting_attention(q.reshape(Bsz, S, N_HEADS, HEAD_DIM),
                             k.reshape(Bsz, S, N_HEADS, HEAD_DIM),
                             v.reshape(Bsz, S, N_HEADS, HEAD_DIM), log_f)
    y_attn = o @ w_attn_out
    g = jax.nn.sigmoid(gates)
    g_conv, g_attn = g[..., :D_MODEL], g[..., D_MODEL:]
    return (g_conv * y_conv + g_attn * y_attn) @ w_o


def swiglu(a, w_gate, w_up, w_down):
    return (jax.nn.silu(a @ w_gate) * (a @ w_up)) @ w_down


def moe_swiglu(a, router_w, router_b, w_gate, w_up, w_down):
    Bsz, S, D = a.shape
    af = a.reshape(-1, D)
    N = af.shape[0]
    logits = (af @ router_w).astype(jnp.float32) + router_b.astype(jnp.float32)
    top_logit, top_idx = lax.top_k(logits, TOP_K)
    top_w = jax.nn.softmax(top_logit, axis=-1)
    NK = N * TOP_K
    flat_e = top_idx.reshape(NK).astype(jnp.int32)
    slot = jnp.arange(NK, dtype=jnp.int32)
    se, order = lax.sort((flat_e, slot), num_keys=1, is_stable=True)
    stok = order // TOP_K
    sw = top_w.reshape(NK)[order]
    counts = jnp.bincount(flat_e, length=N_EXPERTS)
    padded = (counts + MOE_GROUP - 1) // MOE_GROUP * MOE_GROUP
    pad_end = jnp.cumsum(padded)
    pad_start = pad_end - padded
    start = jnp.cumsum(counts) - counts
    dest = pad_start[se] + slot - start[se]
    n_blocks = -(-NK // MOE_GROUP) + N_EXPERTS
    R = n_blocks * MOE_GROUP
    row_tok = jnp.zeros((R,), jnp.int32).at[dest].set(stok)
    row_w = jnp.zeros((R,), jnp.float32).at[dest].set(sw)
    block_start = jnp.arange(n_blocks) * MOE_GROUP
    block_e = jnp.minimum(jnp.searchsorted(pad_end, block_start, side='right'), N_EXPERTS - 1)
    xb = af[row_tok].reshape(n_blocks, MOE_GROUP, D)

    def expert_block(args):
        xg, e = args
        hdn = jax.nn.silu(xg @ w_gate[e]) * (xg @ w_up[e])
        return hdn @ w_down[e]

    yb = lax.map(expert_block, (xb, block_e)).reshape(R, D)
    y = jax.ops.segment_sum(yb.astype(jnp.float32) * row_w[:, None], row_tok, num_segments=N)
    return y.astype(a.dtype).reshape(Bsz, S, D)


def setup_inputs(seed: int = 0) -> dict:
    key = jax.random.key(seed)
    ks = jax.random.split(key, 20)
    f32 = jnp.float32

    def nrm(k, shape, fan_in):
        return jax.random.normal(k, shape, f32) * (fan_in ** -0.5)

    return {
        "x": jax.random.normal(ks[0], (BATCH, SEQ, D_MODEL), f32),
        "mix_norm": 1.0 + 0.02 * jax.random.normal(ks[1], (DEPTH, D_MODEL), f32),
        "w_in": nrm(ks[2], (DEPTH, D_MODEL, P_TOTAL), D_MODEL),
        "b_forget": 4.0 + 0.5 * jax.random.normal(ks[3], (DEPTH, N_HEADS), f32),
        "conv_w": nrm(ks[4], (DEPTH, CONV_K, D_CONV), CONV_K),
        "w_conv_out": nrm(ks[5], (DEPTH, D_CONV, D_MODEL), D_CONV),
        "w_attn_out": nrm(ks[6], (DEPTH, D_ATTN, D_MODEL), D_ATTN),
        "w_o": nrm(ks[7], (DEPTH, D_MODEL, D_MODEL), D_MODEL),
        "ffn_norm": 1.0 + 0.02 * jax.random.normal(ks[8], (DEPTH, D_MODEL), f32),
        "dense_w_gate": nrm(ks[9], (N_DENSE, D_MODEL, D_FF), D_MODEL),
        "dense_w_up": nrm(ks[10], (N_DENSE, D_MODEL, D_FF), D_MODEL),
        "dense_w_down": nrm(ks[11], (N_DENSE, D_FF, D_MODEL), D_FF),
        "router_w": nrm(ks[12], (N_MOE, D_MODEL, N_EXPERTS), D_MODEL),
        "router_b": 0.01 * jax.random.normal(ks[13], (N_MOE, N_EXPERTS), f32),
        "moe_w_gate": nrm(ks[14], (N_MOE, N_EXPERTS, D_MODEL, D_FF_EXPERT), D_MODEL),
        "moe_w_up": nrm(ks[15], (N_MOE, N_EXPERTS, D_MODEL, D_FF_EXPERT), D_MODEL),
        "moe_w_down": nrm(ks[16], (N_MOE, N_EXPERTS, D_FF_EXPERT, D_MODEL), D_FF_EXPERT),
        "final_norm": 1.0 + 0.02 * jax.random.normal(ks[17], (D_MODEL,), f32),
    }


def reference(x, mix_norm, w_in, b_forget, conv_w, w_conv_out, w_attn_out, w_o, ffn_norm,
              dense_w_gate, dense_w_up, dense_w_down, router_w, router_b,
              moe_w_gate, moe_w_up, moe_w_down, final_norm):
    h = x
    for layer in range(DEPTH):
        a = rmsnorm(h, mix_norm[layer])
        h = h + hybrid_mixer(a, w_in[layer], b_forget[layer], conv_w[layer],
                             w_conv_out[layer], w_attn_out[layer], w_o[layer])
        a = rmsnorm(h, ffn_norm[layer])
        if layer % 2 == 0:
            i = layer // 2
            h = h + swiglu(a, dense_w_gate[i], dense_w_up[i], dense_w_down[i])
        else:
            i = layer // 2
            h = h + moe_swiglu(a, router_w[i], router_b[i], moe_w_gate[i], moe_w_up[i], moe_w_down[i])
    return rmsnorm(h, final_norm)
```

```python
import functools
import math

import jax
import jax.numpy as jnp
from jax import lax
from jax.experimental import pallas as pl
from jax.experimental.pallas import tpu as pltpu

F32 = jnp.float32
BF16 = jnp.bfloat16
I32 = jnp.int32

RMS_EPS = 1e-6
TOP_K = 2
LANES = 128
SUBLANES = 8
VMEM_LIMIT = 56 * 1024 * 1024

PROJ_ROWS = 1024
PROJ_COLS = 1024
CONV_COLS = 512
CUM_ROWS = 512
ATTN_Q = 512
ATTN_K = 512
MIX_ROWS = 256
FFN_ROWS = 512
FFN_COLS = 512
MOE_CHUNK = 512
MOE_ROWBLK = 256
MOE_TILE = 1024
MOE_COLS = 512


def _tile(pref, dim):
    t = min(pref, dim)
    assert dim % t == 0, (pref, dim)
    return t


def _params(*sem):
    return pltpu.CompilerParams(dimension_semantics=sem, vmem_limit_bytes=VMEM_LIMIT)


def _dot(a, b):
    return jnp.dot(a, b, preferred_element_type=F32)


def _dot_nt(a, b):
    return lax.dot_general(a, b, (((1,), (1,)), ((), ())), preferred_element_type=F32)


def _split3(x):
    hi = x.astype(BF16)
    r = x - hi.astype(F32)
    mid = r.astype(BF16)
    lo = (r - mid.astype(F32)).astype(BF16)
    return hi, mid, lo


def _dot_f32_by_01(x, onehot):
    hi, mid, lo = _split3(x)
    return _dot(hi, onehot) + _dot(mid, onehot) + _dot(lo, onehot)


def _dot_f32(a, b):
    a1, a2, a3 = _split3(a)
    b1, b2, b3 = _split3(b)
    small = _dot(a1, b3) + _dot(a3, b1) + _dot(a2, b2)
    mid = _dot(a1, b2) + _dot(a2, b1)
    return _dot(a1, b1) + (mid + small)


def _rms(xf, g):
    ms = jnp.mean(xf * xf, axis=-1, keepdims=True)
    return xf * lax.rsqrt(ms + RMS_EPS) * g


def _norm_kernel(x_ref, g_ref, a_ref):
    a_ref[...] = _rms(x_ref[...], g_ref[...]).astype(a_ref.dtype)


def _norm(x, g):
    n, d = x.shape
    tm = _tile(FFN_ROWS, n)
    return pl.pallas_call(
        _norm_kernel,
        grid=(n // tm,),
        in_specs=[pl.BlockSpec((tm, d), lambda i: (i, 0)),
                  pl.BlockSpec((1, d), lambda i: (0, 0))],
        out_specs=pl.BlockSpec((tm, d), lambda i: (i, 0)),
        out_shape=jax.ShapeDtypeStruct((n, d), BF16),
        compiler_params=_params("parallel"),
        name="rmsnorm",
    )(x, g)


def _proj_kernel(a_ref, w_ref, o_ref, *, sigmoid):
    y = _dot(a_ref[...], w_ref[...])
    if sigmoid:
        y = jax.nn.sigmoid(y)
    o_ref[...] = y.astype(o_ref.dtype)


def _proj(a, w, *, sigmoid, name):
    n, k = a.shape
    p = w.shape[1]
    tm = _tile(PROJ_ROWS, n)
    tn = _tile(PROJ_COLS, p)
    return pl.pallas_call(
        functools.partial(_proj_kernel, sigmoid=sigmoid),
        grid=(n // tm, p // tn),
        in_specs=[pl.BlockSpec((tm, k), lambda i, j: (i, 0)),
                  pl.BlockSpec((k, tn), lambda i, j: (0, j))],
        out_specs=pl.BlockSpec((tm, tn), lambda i, j: (i, j)),
        out_shape=jax.ShapeDtypeStruct((n, p), BF16),
        compiler_params=_params("parallel", "arbitrary"),
        name=name,
    )(a, w)


def _conv_kernel(a_ref, wb_ref, wc_ref, wu_ref, cw_ref, z_ref, carry_ref, *, tiles_per_seq, conv_k):
    i = pl.program_id(0)
    j = pl.program_id(1)

    @pl.when(i % tiles_per_seq == 0)
    def _():
        carry_ref[j] = jnp.zeros(carry_ref.shape[1:], F32)

    a = a_ref[...]
    b = _dot(a, wb_ref[...])
    cu = _dot(a, wc_ref[...]) * _dot(a, wu_ref[...])
    tm = cu.shape[0]
    prev = carry_ref[j]
    row = lax.broadcasted_iota(I32, cu.shape, 0)
    cw = cw_ref[...]
    y = cu * cw[conv_k - 1:conv_k]
    for s in range(1, conv_k):
        sh = pltpu.roll(cu, s, 0)
        for r in range(s):
            src = SUBLANES - s + r
            sh = jnp.where(row == r, prev[src:src + 1], sh)
        y = y + sh * cw[conv_k - 1 - s:conv_k - s]
    z_ref[...] = (b * y).astype(z_ref.dtype)
    carry_ref[j] = cu[tm - SUBLANES:tm]


def _conv_front(a, w_bcu, conv_w_pad, *, d_conv, seq, conv_k):
    n, k = a.shape
    tm = _tile(PROJ_ROWS, seq)
    tn = _tile(CONV_COLS, d_conv)
    nj = d_conv // tn
    return pl.pallas_call(
        functools.partial(_conv_kernel, tiles_per_seq=seq // tm, conv_k=conv_k),
        grid=(n // tm, nj),
        in_specs=[pl.BlockSpec((tm, k), lambda i, j: (i, 0)),
                  pl.BlockSpec((k, tn), lambda i, j: (0, j)),
                  pl.BlockSpec((k, tn), lambda i, j: (0, nj + j)),
                  pl.BlockSpec((k, tn), lambda i, j: (0, 2 * nj + j)),
                  pl.BlockSpec((SUBLANES, tn), lambda i, j: (0, j))],
        out_specs=pl.BlockSpec((tm, tn), lambda i, j: (i, j)),
        out_shape=jax.ShapeDtypeStruct((n, d_conv), BF16),
        scratch_shapes=[pltpu.VMEM((nj, SUBLANES, tn), F32)],
        compiler_params=_params("arbitrary", "arbitrary"),
        name="conv_front",
    )(a, w_bcu, w_bcu, w_bcu, conv_w_pad)


def _cum_kernel(a_ref, wf_ref, bf_ref, c_ref, carry_ref):
    si = pl.program_id(1)

    @pl.when(si == 0)
    def _():
        carry_ref[...] = jnp.zeros(carry_ref.shape, F32)

    lf = jax.nn.log_sigmoid(_dot_nt(wf_ref[...], a_ref[...]) + bf_ref[...])
    tm = lf.shape[1]
    r = lax.broadcasted_iota(I32, (tm, tm), 0)
    c = lax.broadcasted_iota(I32, (tm, tm), 1)
    upper = jnp.where(r <= c, 1.0, 0.0).astype(BF16)
    cs = _dot_f32_by_01(lf, upper) + carry_ref[:, 0:1]
    c_ref[0] = cs
    carry_ref[...] = jnp.broadcast_to(cs[:, tm - 1:tm], carry_ref.shape)


def _forget_cumsum(a, wf_t, bf_col, *, batch, seq):
    n, k = a.shape
    hp = wf_t.shape[0]
    tm = _tile(CUM_ROWS, seq)
    ns = seq // tm
    return pl.pallas_call(
        _cum_kernel,
        grid=(batch, ns),
        in_specs=[pl.BlockSpec((tm, k), lambda b, s: (b * ns + s, 0)),
                  pl.BlockSpec((hp, k), lambda b, s: (0, 0)),
                  pl.BlockSpec((hp, 1), lambda b, s: (0, 0))],
        out_specs=pl.BlockSpec((1, hp, tm), lambda b, s: (b, 0, s)),
        out_shape=jax.ShapeDtypeStruct((batch, hp, seq), F32),
        scratch_shapes=[pltpu.VMEM((hp, LANES), F32)],
        compiler_params=_params("arbitrary", "arbitrary"),
        name="forget_cumsum",
    )(a, wf_t, bf_col)


def _attn_kernel(q_ref, k_ref, v_ref, c_ref, o_ref, m_ref, l_ref, acc_ref, *, tq, tk):
    qi = pl.program_id(2)
    q = q_ref[...]
    q0 = pl.multiple_of(qi * tq, tq)
    c0 = c_ref[0, 0, :, pl.ds(q0, LANES)][:, 0:1]
    m_ref[...] = jnp.full(m_ref.shape, -jnp.inf, F32)
    l_ref[...] = jnp.zeros(l_ref.shape, F32)
    acc_ref[...] = jnp.zeros(acc_ref.shape, F32)

    def chunk(j, masked):
        k0 = pl.multiple_of(j * tk, tk)
        kc = k_ref[pl.ds(k0, tk), :]
        vc = v_ref[pl.ds(k0, tk), :]
        ck = c_ref[0, 0, :, pl.ds(k0, tk)]
        s = _dot_nt(q, kc) - (ck - c0)
        if masked:
            qpos = q0 + lax.broadcasted_iota(I32, (tq, tk), 0)
            kpos = k0 + lax.broadcasted_iota(I32, (tq, tk), 1)
            s = jnp.where(kpos <= qpos, s, -jnp.inf)
        m_prev = m_ref[...]
        m_new = jnp.maximum(m_prev, jnp.max(s, axis=-1, keepdims=True))
        alpha = jnp.exp(m_prev - m_new)
        p = jnp.exp(s - m_new)
        l_ref[...] = alpha * l_ref[...] + jnp.sum(p, axis=-1, keepdims=True)
        acc_ref[...] = alpha * acc_ref[...] + _dot(p.astype(BF16), vc)
        m_ref[...] = m_new

    per = tq // tk
    n_full = qi * per

    def body(j, carry):
        chunk(j, False)
        return carry

    lax.fori_loop(0, n_full, body, 0)
    for d in range(per):
        chunk(n_full + d, True)
    o_ref[...] = (acc_ref[...] / l_ref[...]).astype(o_ref.dtype)


def _attention(qkv, c, *, batch, seq, n_heads, head_dim):
    n = qkv.shape[0]
    tq = _tile(ATTN_Q, seq)
    tk = _tile(ATTN_K, tq)
    nq = seq // tq
    c = c.reshape(batch, c.shape[1], 1, seq)
    return pl.pallas_call(
        functools.partial(_attn_kernel, tq=tq, tk=tk),
        grid=(batch, n_heads, nq),
        in_specs=[pl.BlockSpec((tq, head_dim), lambda b, h, i: (b * nq + i, h)),
                  pl.BlockSpec((seq, head_dim), lambda b, h, i: (b, n_heads + h)),
                  pl.BlockSpec((seq, head_dim), lambda b, h, i: (b, 2 * n_heads + h)),
                  pl.BlockSpec((1, 1, 1, seq), lambda b, h, i: (b, h, 0, 0))],
        out_specs=pl.BlockSpec((tq, head_dim), lambda b, h, i: (b * nq + i, h)),
        out_shape=jax.ShapeDtypeStruct((n, n_heads * head_dim), BF16),
        scratch_shapes=[pltpu.VMEM((tq, 1), F32), pltpu.VMEM((tq, 1), F32),
                        pltpu.VMEM((tq, head_dim), F32)],
        compiler_params=_params("parallel", "parallel", "arbitrary"),
        name="forget_attn",
    )(qkv, qkv, qkv, c)


def _mix_kernel(z_ref, o_ref, ga_ref, gb_ref, h_ref, wc_ref, wa_ref, wo_ref, g_ref, hn_ref, a_ref):
    yc = _dot(z_ref[...], wc_ref[...])
    ya = _dot(o_ref[...], wa_ref[...])
    m = ga_ref[...].astype(F32) * yc + gb_ref[...].astype(F32) * ya
    hn = h_ref[...] + _dot(m.astype(BF16), wo_ref[...])
    hn_ref[...] = hn
    a_ref[...] = _rms(hn, g_ref[...]).astype(a_ref.dtype)


def _mix_out(z, o, gates, h, wc, wa, wo, g):
    n, d = h.shape
    dc = z.shape[1]
    da = o.shape[1]
    tm = _tile(MIX_ROWS, n)
    const = lambda i: (0, 0)
    return pl.pallas_call(
        _mix_kernel,
        grid=(n // tm,),
        in_specs=[pl.BlockSpec((tm, dc), lambda i: (i, 0)),
                  pl.BlockSpec((tm, da), lambda i: (i, 0)),
                  pl.BlockSpec((tm, d), lambda i: (i, 0)),
                  pl.BlockSpec((tm, d), lambda i: (i, 1)),
                  pl.BlockSpec((tm, d), lambda i: (i, 0)),
                  pl.BlockSpec((dc, d), const),
                  pl.BlockSpec((da, d), const),
                  pl.BlockSpec((d, d), const),
                  pl.BlockSpec((1, d), const)],
        out_specs=[pl.BlockSpec((tm, d), lambda i: (i, 0)),
                   pl.BlockSpec((tm, d), lambda i: (i, 0))],
        out_shape=[jax.ShapeDtypeStruct((n, d), F32), jax.ShapeDtypeStruct((n, d), BF16)],
        input_output_aliases={4: 0},
        compiler_params=_params("parallel"),
        name="mix_out",
    )(z, o, gates, gates, h, wc, wa, wo, g)


def _epilogue_shapes(n, d, final):
    if final:
        return [jax.ShapeDtypeStruct((n, d), F32)]
    return [jax.ShapeDtypeStruct((n, d), F32), jax.ShapeDtypeStruct((n, d), BF16)]


def _write_epilogue(hn, g, outs, final):
    if final:
        outs[0][...] = _rms(hn, g)
    else:
        outs[0][...] = hn
        outs[1][...] = _rms(hn, g).astype(outs[1].dtype)


def _ffn_kernel(a_ref, wg_ref, wu_ref, wd_ref, h_ref, g_ref, *rest, final):
    outs, acc_ref = rest[:-1], rest[-1]
    f = pl.program_id(1)
    a = a_ref[...]
    hid = (jax.nn.silu(_dot(a, wg_ref[...])) * _dot(a, wu_ref[...])).astype(BF16)
    contrib = _dot(hid, wd_ref[...])

    @pl.when(f == 0)
    def _():
        acc_ref[...] = h_ref[...] + contrib

    @pl.when(f > 0)
    def _():
        acc_ref[...] += contrib

    @pl.when(f == pl.num_programs(1) - 1)
    def _():
        _write_epilogue(acc_ref[...], g_ref[...], outs, final)


def _dense_ffn(a, h, wg, wu, wd, g, *, final):
    n, d = h.shape
    ff = wg.shape[1]
    tm = _tile(FFN_ROWS, n)
    tf = _tile(FFN_COLS, ff)
    row = lambda i, f: (i, 0)
    shapes = _epilogue_shapes(n, d, final)
    return pl.pallas_call(
        functools.partial(_ffn_kernel, final=final),
        grid=(n // tm, ff // tf),
        in_specs=[pl.BlockSpec((tm, d), row),
                  pl.BlockSpec((d, tf), lambda i, f: (0, f)),
                  pl.BlockSpec((d, tf), lambda i, f: (0, f)),
                  pl.BlockSpec((tf, d), lambda i, f: (f, 0)),
                  pl.BlockSpec((tm, d), row),
                  pl.BlockSpec((1, d), lambda i, f: (0, 0))],
        out_specs=[pl.BlockSpec((tm, d), row) for _ in shapes],
        out_shape=shapes,
        scratch_shapes=[pltpu.VMEM((tm, d), F32)],
        compiler_params=_params("parallel", "arbitrary"),
        name="dense_ffn",
    )(a, wg, wu, wd, h, g)


def _router_kernel(h_ref, g_ref, rw_ref, rb_ref, rc_ref, rr_ref, cnt_ref, carry_ref, *, n_exp):
    i = pl.program_id(0)

    @pl.when(i == 0)
    def _():
        carry_ref[...] = jnp.zeros(carry_ref.shape, F32)

    a = _rms(h_ref[...], g_ref[...])
    tm = a.shape[0]
    lane = lax.broadcasted_iota(I32, (tm, LANES), 1)
    lanef = lane.astype(F32)
    logits = jnp.where(lane < n_exp, _dot_f32(a, rw_ref[...]) + rb_ref[...], -jnp.inf)
    big = float(LANES)
    m1 = jnp.max(logits, axis=-1, keepdims=True)
    i1 = jnp.min(jnp.where(logits == m1, lanef, big), axis=-1, keepdims=True)
    rest = jnp.where(lanef == i1, -jnp.inf, logits)
    m2 = jnp.max(rest, axis=-1, keepdims=True)
    i2 = jnp.min(jnp.where(rest == m2, lanef, big), axis=-1, keepdims=True)
    e2 = jnp.exp(m2 - m1)
    w1 = 1.0 / (1.0 + e2)
    w2 = e2 / (1.0 + e2)

    onehot = jnp.where(lanef == i1, 1.0, jnp.where(lanef == i2, 1.0, 0.0))
    r = lax.broadcasted_iota(I32, (tm, tm), 0)
    c = lax.broadcasted_iota(I32, (tm, tm), 1)
    strict_lower = jnp.where(c < r, 1.0, 0.0).astype(BF16)
    before = carry_ref[0:1, :]
    prefix = _dot(strict_lower, onehot.astype(BF16)) + before
    rank1 = jnp.sum(jnp.where(lanef == i1, prefix, 0.0), axis=-1, keepdims=True)
    rank2 = jnp.sum(jnp.where(lanef == i2, prefix, 0.0), axis=-1, keepdims=True)
    after = before + jnp.sum(onehot, axis=0, keepdims=True)

    srow = lax.broadcasted_iota(I32, (SUBLANES, LANES), 0)
    cnt_ref[0] = jnp.where(srow == 0, before, jnp.where(srow == 1, after, 0.0))
    carry_ref[...] = jnp.broadcast_to(after, carry_ref.shape)

    rec = jnp.where(lane == 0, i1, jnp.where(lane == 1, i2, jnp.where(
        lane == 2, rank1, jnp.where(lane == 3, rank2, jnp.where(
            lane == 4, w1, jnp.where(lane == 5, w2, 0.0))))))
    rc_ref[...] = rec
    rr_ref[0] = rec.T[0:SUBLANES, :]


def _router(h, g, rw_pad, rb_pad, *, n_exp):
    n, d = h.shape
    tm = _tile(MOE_CHUNK, n)
    nc = n // tm
    return pl.pallas_call(
        functools.partial(_router_kernel, n_exp=n_exp),
        grid=(nc,),
        in_specs=[pl.BlockSpec((tm, d), lambda i: (i, 0)),
                  pl.BlockSpec((1, d), lambda i: (0, 0)),
                  pl.BlockSpec((d, LANES), lambda i: (0, 0)),
                  pl.BlockSpec((1, LANES), lambda i: (0, 0))],
        out_specs=[pl.BlockSpec((tm, LANES), lambda i: (i, 0)),
                   pl.BlockSpec((1, SUBLANES, tm), lambda i: (i, 0, 0)),
                   pl.BlockSpec((1, SUBLANES, LANES), lambda i: (i, 0, 0))],
        out_shape=[jax.ShapeDtypeStruct((n, LANES), F32),
                   jax.ShapeDtypeStruct((nc, SUBLANES, tm), F32),
                   jax.ShapeDtypeStruct((nc, SUBLANES, LANES), F32)],
        scratch_shapes=[pltpu.VMEM((SUBLANES, LANES), F32)],
        compiler_params=_params("arbitrary"),
        name="moe_router",
    )(h, g, rw_pad, rb_pad)


def _moe_plan(cnt, *, n_exp, n_tiles, n_items):
    tile, rb, ch = MOE_TILE, MOE_ROWBLK, MOE_CHUNK
    cb = cnt[:, 0, :n_exp].astype(I32)
    ca = cnt[:, 1, :n_exp].astype(I32)
    nc = cb.shape[0]
    counts = ca[-1]
    tiles_e = (counts + tile - 1) // tile
    tile_end = jnp.cumsum(tiles_e)
    n_used = tile_end[-1]
    pad_start = (tile_end - tiles_e) * tile
    t_ids = jnp.arange(n_tiles, dtype=I32)
    tile_e = jnp.minimum(jnp.searchsorted(tile_end, t_ids, side="right"), n_exp - 1).astype(I32)
    tile_e = jnp.where(t_ids < n_used, tile_e, tile_e[n_used - 1])

    w_ids = jnp.arange(n_items, dtype=I32)

    per_tile = tile // rb
    nrb = n_tiles * per_tile
    rb_ids = jnp.arange(nrb, dtype=I32)
    rb_e = tile_e[rb_ids // per_tile]
    rb_used = (rb_ids // per_tile) < n_used
    k0 = rb_ids * rb - pad_start[rb_e]
    k1 = jnp.minimum(k0 + rb, counts[rb_e])
    has = rb_used & (k0 < counts[rb_e])
    c_lo = jnp.sum(ca.T[rb_e] <= k0[:, None], axis=1).astype(I32)
    c_hi = jnp.sum(cb.T[rb_e] < k1[:, None], axis=1).astype(I32) - 1
    n_it = jnp.where(has, c_hi - c_lo + 1, jnp.where(rb_used, 1, 0)).astype(I32)
    c_lo = jnp.where(has, c_lo, 0)
    off_end = jnp.cumsum(n_it)
    off_start = off_end - n_it
    total = off_end[-1]
    g_rb = jnp.minimum(jnp.searchsorted(off_end, w_ids, side="right"), nrb - 1).astype(I32)
    g_c = c_lo[g_rb] + w_ids - off_start[g_rb]
    g_valid = w_ids < total
    g_first = g_valid & (w_ids == off_start[g_rb])
    g_last = g_valid & (w_ids == off_end[g_rb] - 1)
    g_rb = jnp.where(g_valid, g_rb, g_rb[total - 1])
    g_c = jnp.where(g_valid, g_c, g_c[total - 1])
    g_fl = g_first.astype(I32) + 2 * g_last.astype(I32) + 4 * g_valid.astype(I32)

    span = ch // rb + 1
    row_lo = pad_start[None, :] + cb
    row_hi = pad_start[None, :] + ca - 1
    nblk = jnp.where(ca > cb, row_hi // rb - row_lo // rb + 1, 0)
    ks = jnp.arange(span, dtype=I32)
    cand_rb = ((row_lo // rb)[..., None] + ks).reshape(-1)
    cand_ok = (ks < nblk[..., None]).reshape(-1)
    idx = jnp.nonzero(cand_ok, size=n_items, fill_value=0)[0].astype(I32)
    total2 = jnp.sum(cand_ok).astype(I32)
    s_valid = w_ids < total2
    s_c = idx // (n_exp * span)
    s_rb = cand_rb[idx]
    s_c = jnp.where(s_valid, s_c, s_c[total2 - 1])
    s_rb = jnp.where(s_valid, s_rb, s_rb[total2 - 1])
    prev_c = jnp.concatenate([jnp.full((1,), -1, I32), s_c[:-1]])
    next_c = jnp.concatenate([s_c[1:], jnp.full((1,), -1, I32)])
    s_first = s_valid & (s_c != prev_c)
    s_last = s_valid & ((s_c != next_c) | (w_ids == total2 - 1))
    s_fl = s_first.astype(I32) + 2 * s_last.astype(I32) + 4 * s_valid.astype(I32)

    return dict(pad_start=pad_start.astype(I32), tile_e=tile_e, n_used=n_used.reshape(1).astype(I32),
                g_rb=g_rb, g_c=g_c, g_fl=g_fl, s_c=s_c.astype(I32), s_rb=s_rb.astype(I32), s_fl=s_fl)


def _dest(e, rank, ps_ref, n_exp):
    base = jnp.zeros_like(rank)
    for x in range(n_exp):
        base = jnp.where(e == float(x), ps_ref[x].astype(F32), base)
    return base + rank


def _gather_kernel(rb_ref, c_ref, fl_ref, ps_ref, rr_ref, a_ref, xb_ref, rw_ref, acc_ref, accw_ref, *, n_exp):
    w = pl.program_id(0)
    fl = fl_ref[w]
    nrow, ch = acc_ref.shape[0], a_ref.shape[0]

    @pl.when((fl & 1) != 0)
    def _():
        acc_ref[...] = jnp.zeros(acc_ref.shape, F32)
        accw_ref[...] = jnp.zeros(accw_ref.shape, F32)

    @pl.when((fl & 4) != 0)
    def _():
        rr = rr_ref[0]
        row0 = (rb_ref[w] * nrow).astype(F32)
        d1 = _dest(rr[0:1], rr[2:3], ps_ref, n_exp) - row0
        d2 = _dest(rr[1:2], rr[3:4], ps_ref, n_exp) - row0
        j = lax.broadcasted_iota(I32, (nrow, ch), 0).astype(F32)
        hit1 = d1 == j
        hit2 = d2 == j
        sel = jnp.where(hit1, 1.0, jnp.where(hit2, 1.0, 0.0)).astype(BF16)
        acc_ref[...] += _dot(sel, a_ref[...])
        wsel = jnp.where(hit1, rr[4:5], jnp.where(hit2, rr[5:6], 0.0))
        accw_ref[...] += jnp.sum(wsel, axis=-1, keepdims=True)

    @pl.when((fl & 2) != 0)
    def _():
        xb_ref[...] = acc_ref[...].astype(xb_ref.dtype)
        rw_ref[...] = accw_ref[...]


def _moe_gather(plan, rr, a, *, n_exp, n_rows):
    n, d = a.shape
    ch, rb = rr.shape[2], MOE_ROWBLK
    n_items = plan["g_rb"].shape[0]
    spec = pltpu.PrefetchScalarGridSpec(
        num_scalar_prefetch=4,
        grid=(n_items,),
        in_specs=[pl.BlockSpec((1, SUBLANES, ch), lambda w, rbi, ci, fl, ps: (ci[w], 0, 0)),
                  pl.BlockSpec((ch, d), lambda w, rbi, ci, fl, ps: (ci[w], 0))],
        out_specs=[pl.BlockSpec((rb, d), lambda w, rbi, ci, fl, ps: (rbi[w], 0)),
                   pl.BlockSpec((rb, LANES), lambda w, rbi, ci, fl, ps: (rbi[w], 0))],
        scratch_shapes=[pltpu.VMEM((rb, d), F32), pltpu.VMEM((rb, LANES), F32)],
    )
    return pl.pallas_call(
        functools.partial(_gather_kernel, n_exp=n_exp),
        grid_spec=spec,
        out_shape=[jax.ShapeDtypeStruct((n_rows, d), BF16),
                   jax.ShapeDtypeStruct((n_rows, LANES), F32)],
        compiler_params=_params("arbitrary"),
        name="moe_gather",
    )(plan["g_rb"], plan["g_c"], plan["g_fl"], plan["pad_start"], rr, a)


def _expert_kernel(te_ref, nu_ref, x_ref, wg_ref, wu_ref, wd_ref, rw_ref, y_ref, acc_ref):
    t = pl.program_id(0)
    f = pl.program_id(1)

    @pl.when(t < nu_ref[0])
    def _():
        x = x_ref[...]
        hid = (jax.nn.silu(_dot(x, wg_ref[0])) * _dot(x, wu_ref[0])).astype(BF16)
        contrib = _dot(hid, wd_ref[0])

        @pl.when(f == 0)
        def _():
            acc_ref[...] = contrib

        @pl.when(f > 0)
        def _():
            acc_ref[...] += contrib

        @pl.when(f == pl.num_programs(1) - 1)
        def _():
            y_ref[...] = (acc_ref[...] * rw_ref[:, 0:1]).astype(y_ref.dtype)


def _moe_experts(plan, xb, roww, wg, wu, wd):
    n_rows, d = xb.shape
    ff = wg.shape[2]
    tile = MOE_TILE
    tf = _tile(MOE_COLS, ff)
    nf = ff // tf
    n_tiles = n_rows // tile

    def row(t, f, te, nu):
        return (jnp.minimum(t, nu[0] - 1), 0)

    def fcol(t, f, nu):
        return jnp.where(t < nu[0], f, nf - 1)

    spec = pltpu.PrefetchScalarGridSpec(
        num_scalar_prefetch=2,
        grid=(n_tiles, nf),
        in_specs=[pl.BlockSpec((tile, d), row),
                  pl.BlockSpec((1, d, tf), lambda t, f, te, nu: (te[t], 0, fcol(t, f, nu))),
                  pl.BlockSpec((1, d, tf), lambda t, f, te, nu: (te[t], 0, fcol(t, f, nu))),
                  pl.BlockSpec((1, tf, d), lambda t, f, te, nu: (te[t], fcol(t, f, nu), 0)),
                  pl.BlockSpec((tile, LANES), row)],
        out_specs=pl.BlockSpec((tile, d), row),
        scratch_shapes=[pltpu.VMEM((tile, d), F32)],
    )
    return pl.pallas_call(
        _expert_kernel,
        grid_spec=spec,
        out_shape=jax.ShapeDtypeStruct((n_rows, d), BF16),
        compiler_params=_params("arbitrary", "arbitrary"),
        name="moe_experts",
    )(plan["tile_e"], plan["n_used"], xb, wg, wu, wd, roww)


def _combine_kernel(c_ref, rb_ref, fl_ref, ps_ref, rc_ref, y_ref, h_ref, g_ref, *rest, n_exp, final):
    outs, acc_ref = rest[:-1], rest[-1]
    w = pl.program_id(0)
    fl = fl_ref[w]
    ch, nrow = acc_ref.shape[0], y_ref.shape[0]

    @pl.when((fl & 1) != 0)
    def _():
        acc_ref[...] = h_ref[...]

    @pl.when((fl & 4) != 0)
    def _():
        rc = rc_ref[...]
        row0 = (rb_ref[w] * nrow).astype(F32)
        d1 = _dest(rc[:, 0:1], rc[:, 2:3], ps_ref, n_exp) - row0
        d2 = _dest(rc[:, 1:2], rc[:, 3:4], ps_ref, n_exp) - row0
        j = lax.broadcasted_iota(I32, (ch, nrow), 1).astype(F32)
        sel = jnp.where(d1 == j, 1.0, jnp.where(d2 == j, 1.0, 0.0)).astype(BF16)
        acc_ref[...] += _dot(sel, y_ref[...])

    @pl.when((fl & 2) != 0)
    def _():
        _write_epilogue(acc_ref[...], g_ref[...], outs, final)


def _moe_combine(plan, rc, yb, h, g, *, n_exp, final):
    n, d = h.shape
    ch, rb = MOE_CHUNK if n >= MOE_CHUNK else n, MOE_ROWBLK
    n_items = plan["s_c"].shape[0]
    tok = lambda w, ci, rbi, fl, ps: (ci[w], 0)
    shapes = _epilogue_shapes(n, d, final)
    spec = pltpu.PrefetchScalarGridSpec(
        num_scalar_prefetch=4,
        grid=(n_items,),
        in_specs=[pl.BlockSpec((ch, LANES), tok),
                  pl.BlockSpec((rb, d), lambda w, ci, rbi, fl, ps: (rbi[w], 0)),
                  pl.BlockSpec((ch, d), tok),
                  pl.BlockSpec((1, d), lambda w, ci, rbi, fl, ps: (0, 0))],
        out_specs=[pl.BlockSpec((ch, d), tok) for _ in shapes],
        scratch_shapes=[pltpu.VMEM((ch, d), F32)],
    )
    return pl.pallas_call(
        functools.partial(_combine_kernel, n_exp=n_exp, final=final),
        grid_spec=spec,
        out_shape=shapes,
        compiler_params=_params("arbitrary"),
        name="moe_combine",
    )(plan["s_c"], plan["s_rb"], plan["s_fl"], plan["pad_start"], rc, yb, h, g)


def _moe_ffn(a, h, g_ffn, g_next, rw, rb, wg, wu, wd, *, final):
    n, d = h.shape
    n_exp = rw.shape[1]
    assert n_exp <= LANES and TOP_K == 2
    ch = _tile(MOE_CHUNK, n)
    assert ch == MOE_CHUNK and MOE_TILE % MOE_ROWBLK == 0
    rw_pad = jnp.zeros((d, LANES), F32).at[:, :n_exp].set(rw)
    rb_pad = jnp.zeros((1, LANES), F32).at[0, :n_exp].set(rb)
    rc, rr, cnt = _router(h, g_ffn, rw_pad, rb_pad, n_exp=n_exp)
    n_tiles = (n * TOP_K) // MOE_TILE + n_exp
    n_items = n_tiles * (MOE_TILE // MOE_ROWBLK) + n_exp * (n // ch)
    plan = _moe_plan(cnt, n_exp=n_exp, n_tiles=n_tiles, n_items=n_items)
    xb, roww = _moe_gather(plan, rr, a, n_exp=n_exp, n_rows=n_tiles * MOE_TILE)
    yb = _moe_experts(plan, xb, roww, wg, wu, wd)
    return _moe_combine(plan, rc, yb, h, g_next, n_exp=n_exp, final=final)


def kernel(x, mix_norm, w_in, b_forget, conv_w, w_conv_out, w_attn_out, w_o, ffn_norm, dense_w_gate, dense_w_up, dense_w_down, router_w, router_b, moe_w_gate, moe_w_up, moe_w_down, final_norm):
    batch, seq, d = x.shape
    depth = mix_norm.shape[0]
    conv_k, d_conv = conv_w.shape[1], conv_w.shape[2]
    d_attn = w_attn_out.shape[1]
    n_heads = b_forget.shape[1]
    head_dim = d_attn // n_heads
    assert head_dim % LANES == 0 and conv_k - 1 <= SUBLANES
    n = batch * seq
    hp = 2 * SUBLANES
    assert n_heads <= hp
    scale = 1.0 / math.sqrt(head_dim)
    o_q = 3 * d_conv
    o_f = o_q + 3 * d_attn
    o_g = o_f + n_heads

    h = x.reshape(n, d)
    a = _norm(h, mix_norm[0].reshape(1, d))
    out = None
    for layer in range(depth):
        wl = w_in[layer]
        w_bcu = wl[:, :o_q].astype(BF16)
        w_qkv = jnp.concatenate([wl[:, o_q:o_q + d_attn] * scale, wl[:, o_q + d_attn:o_f]], axis=1).astype(BF16)
        wf_t = jnp.zeros((hp, d), F32).at[:n_heads].set(wl[:, o_f:o_g].T).astype(BF16)
        bf_col = jnp.zeros((hp, 1), F32).at[:n_heads, 0].set(b_forget[layer])
        w_gates = wl[:, o_g:].astype(BF16)
        cw_pad = jnp.zeros((SUBLANES, d_conv), F32).at[:conv_k].set(conv_w[layer])

        z = _conv_front(a, w_bcu, cw_pad, d_conv=d_conv, seq=seq, conv_k=conv_k)
        qkv = _proj(a, w_qkv, sigmoid=False, name="qkv_proj")
        gates = _proj(a, w_gates, sigmoid=True, name="gate_proj")
        c = _forget_cumsum(a, wf_t, bf_col, batch=batch, seq=seq)
        o = _attention(qkv, c, batch=batch, seq=seq, n_heads=n_heads, head_dim=head_dim)
        h, a = _mix_out(z, o, gates, h, w_conv_out[layer].astype(BF16), w_attn_out[layer].astype(BF16),
                        w_o[layer].astype(BF16), ffn_norm[layer].reshape(1, d))

        final = layer == depth - 1
        g_next = (final_norm if final else mix_norm[layer + 1]).reshape(1, d)
        i = layer // 2
        if layer % 2 == 0:
            res = _dense_ffn(a, h, dense_w_gate[i].astype(BF16), dense_w_up[i].astype(BF16),
                             dense_w_down[i].astype(BF16), g_next, final=final)
        else:
            res = _moe_ffn(a, h, ffn_norm[layer].reshape(1, d), g_next, router_w[i], router_b[i],
                           moe_w_gate[i].astype(BF16), moe_w_up[i].astype(BF16),
                           moe_w_down[i].astype(BF16), final=final)
        if final:
            out = res[0]
        else:
            h, a = res
    return out.reshape(batch, seq, d)
```

```python
import functools
import math

import jax
import jax.numpy as jnp
from jax import lax
from jax.experimental import pallas as pl
from jax.experimental.pallas import tpu as pltpu

F32 = jnp.float32
BF16 = jnp.bfloat16
I32 = jnp.int32

RMS_EPS = 1e-6
LOG2_E = math.log2(math.e)
TOP_K = 2
LANES = 128
SUBLANES = 8
VMEM_LIMIT = 56 * 1024 * 1024

PROJ_ROWS = 1024
PROJ_COLS = 1024
CONV_COLS = 512
CUM_ROWS = 512
ATTN_TILE = 512
MIX_ROWS = 256
FFN_ROWS = 512
FFN_COLS = 512
MOE_CHUNK = 512
MOE_ROWBLK = 256
MOE_TILE = 1024
MOE_COLS = 512


def _tile(pref, dim):
    t = min(pref, dim)
    assert dim % t == 0, (pref, dim)
    return t


def _params(*sem):
    return pltpu.CompilerParams(dimension_semantics=sem, vmem_limit_bytes=VMEM_LIMIT)


def _dot(a, b):
    return jnp.dot(a, b, preferred_element_type=F32)


def _dot_nt(a, b):
    return lax.dot_general(a, b, (((1,), (1,)), ((), ())), preferred_element_type=F32)


def _split3(x):
    hi = x.astype(BF16)
    r = x - hi.astype(F32)
    mid = r.astype(BF16)
    lo = (r - mid.astype(F32)).astype(BF16)
    return hi, mid, lo


def _dot_f32_by_01(x, onehot):
    hi, mid, lo = _split3(x)
    return _dot(hi, onehot) + _dot(mid, onehot) + _dot(lo, onehot)


def _dot_f32(a, b):
    a1, a2, a3 = _split3(a)
    b1, b2, b3 = _split3(b)
    small = _dot(a1, b3) + _dot(a3, b1) + _dot(a2, b2)
    mid = _dot(a1, b2) + _dot(a2, b1)
    return _dot(a1, b1) + (mid + small)


def _rms(xf, g):
    ms = jnp.mean(xf * xf, axis=-1, keepdims=True)
    return xf * lax.rsqrt(ms + RMS_EPS) * g


def _norm_kernel(x_ref, g_ref, a_ref):
    a_ref[...] = _rms(x_ref[...], g_ref[...]).astype(a_ref.dtype)


def _norm(x, g):
    n, d = x.shape
    tm = _tile(FFN_ROWS, n)
    return pl.pallas_call(
        _norm_kernel,
        grid=(n // tm,),
        in_specs=[pl.BlockSpec((tm, d), lambda i: (i, 0)),
                  pl.BlockSpec((1, d), lambda i: (0, 0))],
        out_specs=pl.BlockSpec((tm, d), lambda i: (i, 0)),
        out_shape=jax.ShapeDtypeStruct((n, d), BF16),
        compiler_params=_params("parallel"),
        name="rmsnorm",
    )(x, g)


def _proj_kernel(a_ref, w_ref, o_ref, *, sigmoid):
    y = _dot(a_ref[...], w_ref[...])
    if sigmoid:
        y = jax.nn.sigmoid(y)
    o_ref[...] = y.astype(o_ref.dtype)


def _proj(a, w, *, sigmoid, name):
    n, k = a.shape
    p = w.shape[1]
    tm = _tile(PROJ_ROWS, n)
    tn = _tile(PROJ_COLS, p)
    return pl.pallas_call(
        functools.partial(_proj_kernel, sigmoid=sigmoid),
        grid=(n // tm, p // tn),
        in_specs=[pl.BlockSpec((tm, k), lambda i, j: (i, 0)),
                  pl.BlockSpec((k, tn), lambda i, j: (0, j))],
        out_specs=pl.BlockSpec((tm, tn), lambda i, j: (i, j)),
        out_shape=jax.ShapeDtypeStruct((n, p), BF16),
        compiler_params=_params("parallel", "arbitrary"),
        name=name,
    )(a, w)


def _conv_kernel(a_ref, wb_ref, wc_ref, wu_ref, cw_ref, z_ref, carry_ref, *, tiles_per_seq, conv_k):
    i = pl.program_id(0)
    j = pl.program_id(1)

    @pl.when(i % tiles_per_seq == 0)
    def _():
        carry_ref[j] = jnp.zeros(carry_ref.shape[1:], F32)

    a = a_ref[...]
    b = _dot(a, wb_ref[...])
    cu = _dot(a, wc_ref[...]) * _dot(a, wu_ref[...])
    tm = cu.shape[0]
    prev = carry_ref[j]
    row = lax.broadcasted_iota(I32, cu.shape, 0)
    cw = cw_ref[...]
    y = cu * cw[conv_k - 1:conv_k]
    for s in range(1, conv_k):
        sh = pltpu.roll(cu, s, 0)
        for r in range(s):
            src = SUBLANES - s + r
            sh = jnp.where(row == r, prev[src:src + 1], sh)
        y = y + sh * cw[conv_k - 1 - s:conv_k - s]
    z_ref[...] = (b * y).astype(z_ref.dtype)
    carry_ref[j] = cu[tm - SUBLANES:tm]


def _conv_front(a, w_bcu, conv_w_pad, *, d_conv, seq, conv_k):
    n, k = a.shape
    tm = _tile(PROJ_ROWS, seq)
    tn = _tile(CONV_COLS, d_conv)
    nj = d_conv // tn
    return pl.pallas_call(
        functools.partial(_conv_kernel, tiles_per_seq=seq // tm, conv_k=conv_k),
        grid=(n // tm, nj),
        in_specs=[pl.BlockSpec((tm, k), lambda i, j: (i, 0)),
                  pl.BlockSpec((k, tn), lambda i, j: (0, j)),
                  pl.BlockSpec((k, tn), lambda i, j: (0, nj + j)),
                  pl.BlockSpec((k, tn), lambda i, j: (0, 2 * nj + j)),
                  pl.BlockSpec((SUBLANES, tn), lambda i, j: (0, j))],
        out_specs=pl.BlockSpec((tm, tn), lambda i, j: (i, j)),
        out_shape=jax.ShapeDtypeStruct((n, d_conv), BF16),
        scratch_shapes=[pltpu.VMEM((nj, SUBLANES, tn), F32)],
        compiler_params=_params("arbitrary", "arbitrary"),
        name="conv_front",
    )(a, w_bcu, w_bcu, w_bcu, conv_w_pad)


def _cum_kernel(a_ref, wf_ref, bf_ref, c_ref, carry_ref):
    si = pl.program_id(1)

    @pl.when(si == 0)
    def _():
        carry_ref[...] = jnp.zeros(carry_ref.shape, F32)

    lf = jax.nn.log_sigmoid(_dot_nt(wf_ref[...], a_ref[...]) + bf_ref[...])
    tm = lf.shape[1]
    r = lax.broadcasted_iota(I32, (tm, tm), 0)
    c = lax.broadcasted_iota(I32, (tm, tm), 1)
    upper = jnp.where(r <= c, 1.0, 0.0).astype(BF16)
    cs = _dot_f32_by_01(lf, upper) + carry_ref[:, 0:1]
    c_ref[0] = cs * LOG2_E
    carry_ref[...] = jnp.broadcast_to(cs[:, tm - 1:tm], carry_ref.shape)


def _forget_cumsum(a, wf_t, bf_col, *, batch, seq):
    n, k = a.shape
    hp = wf_t.shape[0]
    tm = _tile(CUM_ROWS, seq)
    ns = seq // tm
    return pl.pallas_call(
        _cum_kernel,
        grid=(batch, ns),
        in_specs=[pl.BlockSpec((tm, k), lambda b, s: (b * ns + s, 0)),
                  pl.BlockSpec((hp, k), lambda b, s: (0, 0)),
                  pl.BlockSpec((hp, 1), lambda b, s: (0, 0))],
        out_specs=pl.BlockSpec((1, hp, tm), lambda b, s: (b, 0, s)),
        out_shape=jax.ShapeDtypeStruct((batch, hp, seq), F32),
        scratch_shapes=[pltpu.VMEM((hp, LANES), F32)],
        compiler_params=_params("arbitrary", "arbitrary"),
        name="forget_cumsum",
    )(a, wf_t, bf_col)


def _attn_kernel(q_ref, k_ref, v_ref, c_ref, o_ref, s_ref, p_ref, al_ref, m_ref, acc_ref, *, t):
    qi = pl.program_id(2)
    dh = q_ref.shape[1]
    reps = t // LANES
    q0 = pl.multiple_of(qi * t, t)
    c0 = c_ref[0, 0, :, pl.ds(q0, LANES)][:, 0:1]
    ones_col = jnp.where(lax.broadcasted_iota(I32, (t, LANES), 1) == 0, 1.0, 0.0).astype(BF16)

    def score(slot, k0, masked):
        s = _dot_nt(q_ref[...], k_ref[pl.ds(k0, t), :]) - (c_ref[0, 0, :, pl.ds(k0, t)] - c0)
        if masked:
            row = lax.broadcasted_iota(I32, (t, t), 0)
            col = lax.broadcasted_iota(I32, (t, t), 1)
            s = jnp.where(col <= row, s, -jnp.inf)
        s_ref[slot] = s

    def softmax(slot):
        s = s_ref[slot]
        m_prev = m_ref[...]
        m_new = jnp.maximum(m_prev, jnp.max(s, axis=-1, keepdims=True))
        al_ref[slot] = jnp.exp2(m_prev - m_new)
        p_ref[slot] = jnp.exp2(s - jnp.concatenate([m_new] * reps, axis=1)).astype(BF16)
        m_ref[...] = m_new

    def apply(slot, k0):
        vaug = jnp.concatenate([v_ref[pl.ds(k0, t), :], ones_col], axis=1)
        al = jnp.concatenate([al_ref[slot]] * (acc_ref.shape[1] // LANES), axis=1)
        acc_ref[...] = al * acc_ref[...] + _dot(p_ref[slot], vaug)

    def key_start(item):
        return pl.multiple_of(jnp.where(item == 0, q0, jnp.maximum(item - 1, 0) * t), t)

    m_ref[...] = jnp.full(m_ref.shape, -jnp.inf, F32)
    acc_ref[...] = jnp.zeros(acc_ref.shape, F32)
    p_ref[1] = jnp.zeros(p_ref.shape[1:], BF16)
    al_ref[1] = jnp.ones(al_ref.shape[1:], F32)
    score(0, q0, True)

    def step(i, slot):
        score(1 - slot, pl.multiple_of(i * t, t), False)
        softmax(slot)
        apply(1 - slot, key_start(i - 1))

    def drain(last, slot):
        softmax(slot)
        apply(1 - slot, key_start(last - 1))
        apply(slot, key_start(last))

    def body(i2, carry):
        step(2 * i2, 0)
        step(2 * i2 + 1, 1)
        return carry

    last = qi
    lax.fori_loop(0, last // 2, body, 0)

    @pl.when(last % 2 == 1)
    def _():
        step(last - 1, 0)
        drain(last, 1)

    @pl.when(last % 2 == 0)
    def _():
        drain(last, 0)

    acc = acc_ref[...]
    o_ref[...] = (acc[:, :dh] / acc[:, dh:dh + 1]).astype(o_ref.dtype)


def _attention(qkv, c, *, batch, seq, n_heads, head_dim):
    n = qkv.shape[0]
    t = _tile(ATTN_TILE, seq)
    nq = seq // t
    c = c.reshape(batch, c.shape[1], 1, seq)
    return pl.pallas_call(
        functools.partial(_attn_kernel, t=t),
        grid=(batch, n_heads, nq),
        in_specs=[pl.BlockSpec((t, head_dim), lambda b, h, i: (b * nq + i, h)),
                  pl.BlockSpec((seq, head_dim), lambda b, h, i: (b, n_heads + h)),
                  pl.BlockSpec((seq, head_dim), lambda b, h, i: (b, 2 * n_heads + h)),
                  pl.BlockSpec((1, 1, 1, seq), lambda b, h, i: (b, h, 0, 0))],
        out_specs=pl.BlockSpec((t, head_dim), lambda b, h, i: (b * nq + i, h)),
        out_shape=jax.ShapeDtypeStruct((n, n_heads * head_dim), BF16),
        scratch_shapes=[pltpu.VMEM((2, t, t), F32), pltpu.VMEM((2, t, t), BF16),
                        pltpu.VMEM((2, t, LANES), F32), pltpu.VMEM((t, LANES), F32),
                        pltpu.VMEM((t, head_dim + LANES), F32)],
        compiler_params=_params("parallel", "parallel", "arbitrary"),
        name="forget_attn",
    )(qkv, qkv, qkv, c)


def _mix_kernel(z_ref, o_ref, ga_ref, gb_ref, h_ref, wc_ref, wa_ref, wo_ref, g_ref, hn_ref, a_ref):
    yc = _dot(z_ref[...], wc_ref[...])
    ya = _dot(o_ref[...], wa_ref[...])
    m = ga_ref[...].astype(F32) * yc + gb_ref[...].astype(F32) * ya
    hn = h_ref[...] + _dot(m.astype(BF16), wo_ref[...])
    hn_ref[...] = hn
    a_ref[...] = _rms(hn, g_ref[...]).astype(a_ref.dtype)


def _mix_out(z, o, gates, h, wc, wa, wo, g):
    n, d = h.shape
    dc = z.shape[1]
    da = o.shape[1]
    tm = _tile(MIX_ROWS, n)
    const = lambda i: (0, 0)
    return pl.pallas_call(
        _mix_kernel,
        grid=(n // tm,),
        in_specs=[pl.BlockSpec((tm, dc), lambda i: (i, 0)),
                  pl.BlockSpec((tm, da), lambda i: (i, 0)),
                  pl.BlockSpec((tm, d), lambda i: (i, 0)),
                  pl.BlockSpec((tm, d), lambda i: (i, 1)),
                  pl.BlockSpec((tm, d), lambda i: (i, 0)),
                  pl.BlockSpec((dc, d), const),
                  pl.BlockSpec((da, d), const),
                  pl.BlockSpec((d, d), const),
                  pl.BlockSpec((1, d), const)],
        out_specs=[pl.BlockSpec((tm, d), lambda i: (i, 0)),
                   pl.BlockSpec((tm, d), lambda i: (i, 0))],
        out_shape=[jax.ShapeDtypeStruct((n, d), F32), jax.ShapeDtypeStruct((n, d), BF16)],
        input_output_aliases={4: 0},
        compiler_params=_params("parallel"),
        name="mix_out",
    )(z, o, gates, gates, h, wc, wa, wo, g)


def _epilogue_shapes(n, d, final):
    if final:
        return [jax.ShapeDtypeStruct((n, d), F32)]
    return [jax.ShapeDtypeStruct((n, d), F32), jax.ShapeDtypeStruct((n, d), BF16)]


def _write_epilogue(hn, g, outs, final):
    if final:
        outs[0][...] = _rms(hn, g)
    else:
        outs[0][...] = hn
        outs[1][...] = _rms(hn, g).astype(outs[1].dtype)


def _ffn_kernel(a_ref, wg_ref, wu_ref, wd_ref, h_ref, g_ref, *rest, final):
    outs, acc_ref = rest[:-1], rest[-1]
    f = pl.program_id(1)
    a = a_ref[...]
    hid = (jax.nn.silu(_dot(a, wg_ref[...])) * _dot(a, wu_ref[...])).astype(BF16)
    contrib = _dot(hid, wd_ref[...])

    @pl.when(f == 0)
    def _():
        acc_ref[...] = h_ref[...] + contrib

    @pl.when(f > 0)
    def _():
        acc_ref[...] += contrib

    @pl.when(f == pl.num_programs(1) - 1)
    def _():
        _write_epilogue(acc_ref[...], g_ref[...], outs, final)


def _dense_ffn(a, h, wg, wu, wd, g, *, final):
    n, d = h.shape
    ff = wg.shape[1]
    tm = _tile(FFN_ROWS, n)
    tf = _tile(FFN_COLS, ff)
    row = lambda i, f: (i, 0)
    shapes = _epilogue_shapes(n, d, final)
    return pl.pallas_call(
        functools.partial(_ffn_kernel, final=final),
        grid=(n // tm, ff // tf),
        in_specs=[pl.BlockSpec((tm, d), row),
                  pl.BlockSpec((d, tf), lambda i, f: (0, f)),
                  pl.BlockSpec((d, tf), lambda i, f: (0, f)),
                  pl.BlockSpec((tf, d), lambda i, f: (f, 0)),
                  pl.BlockSpec((tm, d), row),
                  pl.BlockSpec((1, d), lambda i, f: (0, 0))],
        out_specs=[pl.BlockSpec((tm, d), row) for _ in shapes],
        out_shape=shapes,
        scratch_shapes=[pltpu.VMEM((tm, d), F32)],
        compiler_params=_params("parallel", "arbitrary"),
        name="dense_ffn",
    )(a, wg, wu, wd, h, g)


def _router_kernel(h_ref, g_ref, rw_ref, rb_ref, rc_ref, rr_ref, cnt_ref, carry_ref, *, n_exp):
    i = pl.program_id(0)

    @pl.when(i == 0)
    def _():
        carry_ref[...] = jnp.zeros(carry_ref.shape, F32)

    a = _rms(h_ref[...], g_ref[...])
    tm = a.shape[0]
    lane = lax.broadcasted_iota(I32, (tm, LANES), 1)
    lanef = lane.astype(F32)
    logits = jnp.where(lane < n_exp, _dot_f32(a, rw_ref[...]) + rb_ref[...], -jnp.inf)
    big = float(LANES)
    m1 = jnp.max(logits, axis=-1, keepdims=True)
    i1 = jnp.min(jnp.where(logits == m1, lanef, big), axis=-1, keepdims=True)
    rest = jnp.where(lanef == i1, -jnp.inf, logits)
    m2 = jnp.max(rest, axis=-1, keepdims=True)
    i2 = jnp.min(jnp.where(rest == m2, lanef, big), axis=-1, keepdims=True)
    e2 = jnp.exp(m2 - m1)
    w1 = 1.0 / (1.0 + e2)
    w2 = e2 / (1.0 + e2)

    onehot = jnp.where(lanef == i1, 1.0, jnp.where(lanef == i2, 1.0, 0.0))
    r = lax.broadcasted_iota(I32, (tm, tm), 0)
    c = lax.broadcasted_iota(I32, (tm, tm), 1)
    strict_lower = jnp.where(c < r, 1.0, 0.0).astype(BF16)
    before = carry_ref[0:1, :]
    prefix = _dot(strict_lower, onehot.astype(BF16)) + before
    rank1 = jnp.sum(jnp.where(lanef == i1, prefix, 0.0), axis=-1, keepdims=True)
    rank2 = jnp.sum(jnp.where(lanef == i2, prefix, 0.0), axis=-1, keepdims=True)
    after = before + jnp.sum(onehot, axis=0, keepdims=True)

    srow = lax.broadcasted_iota(I32, (SUBLANES, LANES), 0)
    cnt_ref[0] = jnp.where(srow == 0, before, jnp.where(srow == 1, after, 0.0))
    carry_ref[...] = jnp.broadcast_to(after, carry_ref.shape)

    rec = jnp.where(lane == 0, i1, jnp.where(lane == 1, i2, jnp.where(
        lane == 2, rank1, jnp.where(lane == 3, rank2, jnp.where(
            lane == 4, w1, jnp.where(lane == 5, w2, 0.0))))))
    rc_ref[...] = rec
    rr_ref[0] = rec.T[0:SUBLANES, :]


def _router(h, g, rw_pad, rb_pad, *, n_exp):
    n, d = h.shape
    tm = _tile(MOE_CHUNK, n)
    nc = n // tm
    return pl.pallas_call(
        functools.partial(_router_kernel, n_exp=n_exp),
        grid=(nc,),
        in_specs=[pl.BlockSpec((tm, d), lambda i: (i, 0)),
                  pl.BlockSpec((1, d), lambda i: (0, 0)),
                  pl.BlockSpec((d, LANES), lambda i: (0, 0)),
                  pl.BlockSpec((1, LANES), lambda i: (0, 0))],
        out_specs=[pl.BlockSpec((tm, LANES), lambda i: (i, 0)),
                   pl.BlockSpec((1, SUBLANES, tm), lambda i: (i, 0, 0)),
                   pl.BlockSpec((1, SUBLANES, LANES), lambda i: (i, 0, 0))],
        out_shape=[jax.ShapeDtypeStruct((n, LANES), F32),
                   jax.ShapeDtypeStruct((nc, SUBLANES, tm), F32),
                   jax.ShapeDtypeStruct((nc, SUBLANES, LANES), F32)],
        scratch_shapes=[pltpu.VMEM((SUBLANES, LANES), F32)],
        compiler_params=_params("arbitrary"),
        name="moe_router",
    )(h, g, rw_pad, rb_pad)


def _moe_plan(cnt, *, n_exp, n_tiles, n_items):
    tile, rb, ch = MOE_TILE, MOE_ROWBLK, MOE_CHUNK
    cb = cnt[:, 0, :n_exp].astype(I32)
    ca = cnt[:, 1, :n_exp].astype(I32)
    nc = cb.shape[0]
    counts = ca[-1]
    tiles_e = (counts + tile - 1) // tile
    tile_end = jnp.cumsum(tiles_e)
    n_used = tile_end[-1]
    pad_start = (tile_end - tiles_e) * tile
    t_ids = jnp.arange(n_tiles, dtype=I32)
    tile_e = jnp.minimum(jnp.searchsorted(tile_end, t_ids, side="right"), n_exp - 1).astype(I32)
    tile_e = jnp.where(t_ids < n_used, tile_e, tile_e[n_used - 1])

    w_ids = jnp.arange(n_items, dtype=I32)

    per_tile = tile // rb
    nrb = n_tiles * per_tile
    rb_ids = jnp.arange(nrb, dtype=I32)
    rb_e = tile_e[rb_ids // per_tile]
    rb_used = (rb_ids // per_tile) < n_used
    k0 = rb_ids * rb - pad_start[rb_e]
    k1 = jnp.minimum(k0 + rb, counts[rb_e])
    has = rb_used & (k0 < counts[rb_e])
    c_lo = jnp.sum(ca.T[rb_e] <= k0[:, None], axis=1).astype(I32)
    c_hi = jnp.sum(cb.T[rb_e] < k1[:, None], axis=1).astype(I32) - 1
    n_it = jnp.where(has, c_hi - c_lo + 1, 1).astype(I32)
    c_lo = jnp.where(has, c_lo, 0)
    off_end = jnp.cumsum(n_it)
    off_start = off_end - n_it
    total = off_end[-1]
    g_rb = jnp.minimum(jnp.searchsorted(off_end, w_ids, side="right"), nrb - 1).astype(I32)
    g_c = c_lo[g_rb] + w_ids - off_start[g_rb]
    g_valid = w_ids < total
    g_first = g_valid & (w_ids == off_start[g_rb])
    g_last = g_valid & (w_ids == off_end[g_rb] - 1)
    g_rb = jnp.where(g_valid, g_rb, g_rb[total - 1])
    g_c = jnp.where(g_valid, g_c, g_c[total - 1])
    g_fl = g_first.astype(I32) + 2 * g_last.astype(I32) + 4 * g_valid.astype(I32)

    span = ch // rb + 1
    row_lo = pad_start[None, :] + cb
    row_hi = pad_start[None, :] + ca - 1
    nblk = jnp.where(ca > cb, row_hi // rb - row_lo // rb + 1, 0)
    ks = jnp.arange(span, dtype=I32)
    cand_rb = ((row_lo // rb)[..., None] + ks).reshape(-1)
    cand_ok = (ks < nblk[..., None]).reshape(-1)
    idx = jnp.nonzero(cand_ok, size=n_items, fill_value=0)[0].astype(I32)
    total2 = jnp.sum(cand_ok).astype(I32)
    s_valid = w_ids < total2
    s_c = idx // (n_exp * span)
    s_rb = cand_rb[idx]
    s_c = jnp.where(s_valid, s_c, s_c[total2 - 1])
    s_rb = jnp.where(s_valid, s_rb, s_rb[total2 - 1])
    prev_c = jnp.concatenate([jnp.full((1,), -1, I32), s_c[:-1]])
    next_c = jnp.concatenate([s_c[1:], jnp.full((1,), -1, I32)])
    s_first = s_valid & (s_c != prev_c)
    s_last = s_valid & ((s_c != next_c) | (w_ids == total2 - 1))
    s_fl = s_first.astype(I32) + 2 * s_last.astype(I32) + 4 * s_valid.astype(I32)

    return dict(pad_start=pad_start.astype(I32), tile_e=tile_e, n_used=n_used.reshape(1).astype(I32),
                g_rb=g_rb, g_c=g_c, g_fl=g_fl, s_c=s_c.astype(I32), s_rb=s_rb.astype(I32), s_fl=s_fl)


def _dest(e, rank, ps_ref, n_exp):
    base = jnp.zeros_like(rank)
    for x in range(n_exp):
        base = jnp.where(e == float(x), ps_ref[x].astype(F32), base)
    return base + rank


def _gather_kernel(rb_ref, c_ref, fl_ref, ps_ref, rr_ref, a_ref, xb_ref, rw_ref, acc_ref, accw_ref, *, n_exp):
    w = pl.program_id(0)
    fl = fl_ref[w]
    nrow, ch = acc_ref.shape[0], a_ref.shape[0]

    @pl.when((fl & 1) != 0)
    def _():
        acc_ref[...] = jnp.zeros(acc_ref.shape, F32)
        accw_ref[...] = jnp.zeros(accw_ref.shape, F32)

    @pl.when((fl & 4) != 0)
    def _():
        rr = rr_ref[0]
        row0 = (rb_ref[w] * nrow).astype(F32)
        d1 = _dest(rr[0:1], rr[2:3], ps_ref, n_exp) - row0
        d2 = _dest(rr[1:2], rr[3:4], ps_ref, n_exp) - row0
        j = lax.broadcasted_iota(I32, (nrow, ch), 0).astype(F32)
        hit1 = d1 == j
        hit2 = d2 == j
        sel = jnp.where(hit1, 1.0, jnp.where(hit2, 1.0, 0.0)).astype(BF16)
        acc_ref[...] += _dot(sel, a_ref[...])
        wsel = jnp.where(hit1, rr[4:5], jnp.where(hit2, rr[5:6], 0.0))
        accw_ref[...] += jnp.sum(wsel, axis=-1, keepdims=True)

    @pl.when((fl & 2) != 0)
    def _():
        xb_ref[...] = acc_ref[...].astype(xb_ref.dtype)
        rw_ref[...] = accw_ref[...]


def _moe_gather(plan, rr, a, *, n_exp, n_rows):
    n, d = a.shape
    ch, rb = rr.shape[2], MOE_ROWBLK
    n_items = plan["g_rb"].shape[0]
    spec = pltpu.PrefetchScalarGridSpec(
        num_scalar_prefetch=4,
        grid=(n_items,),
        in_specs=[pl.BlockSpec((1, SUBLANES, ch), lambda w, rbi, ci, fl, ps: (ci[w], 0, 0)),
                  pl.BlockSpec((ch, d), lambda w, rbi, ci, fl, ps: (ci[w], 0))],
        out_specs=[pl.BlockSpec((rb, d), lambda w, rbi, ci, fl, ps: (rbi[w], 0)),
                   pl.BlockSpec((rb, LANES), lambda w, rbi, ci, fl, ps: (rbi[w], 0))],
        scratch_shapes=[pltpu.VMEM((rb, d), F32), pltpu.VMEM((rb, LANES), F32)],
    )
    return pl.pallas_call(
        functools.partial(_gather_kernel, n_exp=n_exp),
        grid_spec=spec,
        out_shape=[jax.ShapeDtypeStruct((n_rows, d), BF16),
                   jax.ShapeDtypeStruct((n_rows, LANES), F32)],
        compiler_params=_params("arbitrary"),
        name="moe_gather",
    )(plan["g_rb"], plan["g_c"], plan["g_fl"], plan["pad_start"], rr, a)


def _expert_kernel(te_ref, nu_ref, x_ref, wg_ref, wu_ref, wd_ref, rw_ref, y_ref, acc_ref):
    t = pl.program_id(0)
    f = pl.program_id(1)

    @pl.when(t < nu_ref[0])
    def _():
        x = x_ref[...]
        hid = (jax.nn.silu(_dot(x, wg_ref[0])) * _dot(x, wu_ref[0])).astype(BF16)
        contrib = _dot(hid, wd_ref[0])

        @pl.when(f == 0)
        def _():
            acc_ref[...] = contrib

        @pl.when(f > 0)
        def _():
            acc_ref[...] += contrib

        @pl.when(f == pl.num_programs(1) - 1)
        def _():
            y_ref[...] = (acc_ref[...] * rw_ref[:, 0:1]).astype(y_ref.dtype)

    @pl.when((t >= nu_ref[0]) & (f == pl.num_programs(1) - 1))
    def _():
        y_ref[...] = jnp.zeros(y_ref.shape, y_ref.dtype)


def _moe_experts(plan, xb, roww, wg, wu, wd):
    n_rows, d = xb.shape
    ff = wg.shape[2]
    tile = MOE_TILE
    tf = _tile(MOE_COLS, ff)
    nf = ff // tf
    n_tiles = n_rows // tile

    def row(t, f, te, nu):
        return (jnp.minimum(t, nu[0] - 1), 0)

    def fcol(t, f, nu):
        return jnp.where(t < nu[0], f, nf - 1)

    spec = pltpu.PrefetchScalarGridSpec(
        num_scalar_prefetch=2,
        grid=(n_tiles, nf),
        in_specs=[pl.BlockSpec((tile, d), row),
                  pl.BlockSpec((1, d, tf), lambda t, f, te, nu: (te[t], 0, fcol(t, f, nu))),
                  pl.BlockSpec((1, d, tf), lambda t, f, te, nu: (te[t], 0, fcol(t, f, nu))),
                  pl.BlockSpec((1, tf, d), lambda t, f, te, nu: (te[t], fcol(t, f, nu), 0)),
                  pl.BlockSpec((tile, LANES), row)],
        out_specs=pl.BlockSpec((tile, d), lambda t, f, te, nu: (t, 0)),
        scratch_shapes=[pltpu.VMEM((tile, d), F32)],
    )
    return pl.pallas_call(
        _expert_kernel,
        grid_spec=spec,
        out_shape=jax.ShapeDtypeStruct((n_rows, d), BF16),
        compiler_params=_params("arbitrary", "arbitrary"),
        name="moe_experts",
    )(plan["tile_e"], plan["n_used"], xb, wg, wu, wd, roww)


def _combine_kernel(c_ref, rb_ref, fl_ref, ps_ref, rc_ref, y_ref, h_ref, g_ref, *rest, n_exp, final):
    outs, acc_ref = rest[:-1], rest[-1]
    w = pl.program_id(0)
    fl = fl_ref[w]
    ch, nrow = acc_ref.shape[0], y_ref.shape[0]

    @pl.when((fl & 1) != 0)
    def _():
        acc_ref[...] = h_ref[...]

    @pl.when((fl & 4) != 0)
    def _():
        rc = rc_ref[...]
        row0 = (rb_ref[w] * nrow).astype(F32)
        d1 = _dest(rc[:, 0:1], rc[:, 2:3], ps_ref, n_exp) - row0
        d2 = _dest(rc[:, 1:2], rc[:, 3:4], ps_ref, n_exp) - row0
        j = lax.broadcasted_iota(I32, (ch, nrow), 1).astype(F32)
        sel = jnp.where(d1 == j, 1.0, jnp.where(d2 == j, 1.0, 0.0)).astype(BF16)
        acc_ref[...] += _dot(sel, y_ref[...])

    @pl.when((fl & 2) != 0)
    def _():
        _write_epilogue(acc_ref[...], g_ref[...], outs, final)


def _moe_combine(plan, rc, yb, h, g, *, n_exp, final):
    n, d = h.shape
    ch, rb = MOE_CHUNK if n >= MOE_CHUNK else n, MOE_ROWBLK
    n_items = plan["s_c"].shape[0]
    tok = lambda w, ci, rbi, fl, ps: (ci[w], 0)
    shapes = _epilogue_shapes(n, d, final)
    spec = pltpu.PrefetchScalarGridSpec(
        num_scalar_prefetch=4,
        grid=(n_items,),
        in_specs=[pl.BlockSpec((ch, LANES), tok),
                  pl.BlockSpec((rb, d), lambda w, ci, rbi, fl, ps: (rbi[w], 0)),
                  pl.BlockSpec((ch, d), tok),
                  pl.BlockSpec((1, d), lambda w, ci, rbi, fl, ps: (0, 0))],
        out_specs=[pl.BlockSpec((ch, d), tok) for _ in shapes],
        scratch_shapes=[pltpu.VMEM((ch, d), F32)],
    )
    return pl.pallas_call(
        functools.partial(_combine_kernel, n_exp=n_exp, final=final),
        grid_spec=spec,
        out_shape=shapes,
        compiler_params=_params("arbitrary"),
        name="moe_combine",
    )(plan["s_c"], plan["s_rb"], plan["s_fl"], plan["pad_start"], rc, yb, h, g)


def _moe_ffn(a, h, g_ffn, g_next, rw, rb, wg, wu, wd, *, final):
    n, d = h.shape
    n_exp = rw.shape[1]
    assert n_exp <= LANES and TOP_K == 2
    ch = _tile(MOE_CHUNK, n)
    assert ch == MOE_CHUNK and MOE_TILE % MOE_ROWBLK == 0
    rw_pad = jnp.zeros((d, LANES), F32).at[:, :n_exp].set(rw)
    rb_pad = jnp.zeros((1, LANES), F32).at[0, :n_exp].set(rb)
    rc, rr, cnt = _router(h, g_ffn, rw_pad, rb_pad, n_exp=n_exp)
    n_tiles = (n * TOP_K) // MOE_TILE + n_exp
    n_items = n_tiles * (MOE_TILE // MOE_ROWBLK) + n_exp * (n // ch)
    plan = _moe_plan(cnt, n_exp=n_exp, n_tiles=n_tiles, n_items=n_items)
    xb, roww = _moe_gather(plan, rr, a, n_exp=n_exp, n_rows=n_tiles * MOE_TILE)
    yb = _moe_experts(plan, xb, roww, wg, wu, wd)
    return _moe_combine(plan, rc, yb, h, g_next, n_exp=n_exp, final=final)


def kernel(x, mix_norm, w_in, b_forget, conv_w, w_conv_out, w_attn_out, w_o, ffn_norm, dense_w_gate, dense_w_up, dense_w_down, router_w, router_b, moe_w_gate, moe_w_up, moe_w_down, final_norm):
    batch, seq, d = x.shape
    depth = mix_norm.shape[0]
    conv_k, d_conv = conv_w.shape[1], conv_w.shape[2]
    d_attn = w_attn_out.shape[1]
    n_heads = b_forget.shape[1]
    head_dim = d_attn // n_heads
    assert head_dim % LANES == 0 and conv_k - 1 <= SUBLANES
    n = batch * seq
    hp = 2 * SUBLANES
    assert n_heads <= hp
    scale = LOG2_E / math.sqrt(head_dim)
    o_q = 3 * d_conv
    o_f = o_q + 3 * d_attn
    o_g = o_f + n_heads

    h = x.reshape(n, d)
    a = _norm(h, mix_norm[0].reshape(1, d))
    out = None
    for layer in range(depth):
        wl = w_in[layer]
        w_bcu = wl[:, :o_q].astype(BF16)
        w_qkv = jnp.concatenate([wl[:, o_q:o_q + d_attn] * scale, wl[:, o_q + d_attn:o_f]], axis=1).astype(BF16)
        wf_t = jnp.zeros((hp, d), F32).at[:n_heads].set(wl[:, o_f:o_g].T).astype(BF16)
        bf_col = jnp.zeros((hp, 1), F32).at[:n_heads, 0].set(b_forget[layer])
        w_gates = wl[:, o_g:].astype(BF16)
        cw_pad = jnp.zeros((SUBLANES, d_conv), F32).at[:conv_k].set(conv_w[layer])

        z = _conv_front(a, w_bcu, cw_pad, d_conv=d_conv, seq=seq, conv_k=conv_k)
        qkv = _proj(a, w_qkv, sigmoid=False, name="qkv_proj")
        gates = _proj(a, w_gates, sigmoid=True, name="gate_proj")
        c = _forget_cumsum(a, wf_t, bf_col, batch=batch, seq=seq)
        o = _attention(qkv, c, batch=batch, seq=seq, n_heads=n_heads, head_dim=head_dim)
        h, a = _mix_out(z, o, gates, h, w_conv_out[layer].astype(BF16), w_attn_out[layer].astype(BF16),
                        w_o[layer].astype(BF16), ffn_norm[layer].reshape(1, d))

        final = layer == depth - 1
        g_next = (final_norm if final else mix_norm[layer + 1]).reshape(1, d)
        i = layer // 2
        if layer % 2 == 0:
            res = _dense_ffn(a, h, dense_w_gate[i].astype(BF16), dense_w_up[i].astype(BF16),
                             dense_w_down[i].astype(BF16), g_next, final=final)
        else:
            res = _moe_ffn(a, h, ffn_norm[layer].reshape(1, d), g_next, router_w[i], router_b[i],
                           moe_w_gate[i].astype(BF16), moe_w_up[i].astype(BF16),
                           moe_w_down[i].astype(BF16), final=final)
        if final:
            out = res[0]
        else:
            h, a = res
    return out.reshape(batch, seq, d)
```

```python
import functools
import math

import jax
import jax.numpy as jnp
from jax import lax
from jax.experimental import pallas as pl
from jax.experimental.pallas import tpu as pltpu

F32 = jnp.float32
BF16 = jnp.bfloat16
I32 = jnp.int32

RMS_EPS = 1e-6
LOG2_E = math.log2(math.e)
TOP_K = 2
LANES = 128
SUBLANES = 8
VMEM_LIMIT = 60 * 1024 * 1024

PROJ_ROWS = 1024
PROJ_COLS = 1024
CONV_COLS = 512
CUM_ROWS = 512
ATTN_TILE = 512
ATTN_UNROLL = 4
MIX_ROWS = 256
FFN_ROWS = 512
FFN_COLS = 512
MOE_CHUNK = 512
MOE_ROWBLK = 256
MOE_TILE = 1024
MOE_COLS = 1024


def _tile(pref, dim):
    t = min(pref, dim)
    assert dim % t == 0, (pref, dim)
    return t


def _params(*sem):
    return pltpu.CompilerParams(dimension_semantics=sem, vmem_limit_bytes=VMEM_LIMIT)


def _dot(a, b):
    return jnp.dot(a, b, preferred_element_type=F32)


def _dot_nt(a, b):
    return lax.dot_general(a, b, (((1,), (1,)), ((), ())), preferred_element_type=F32)


def _split3(x):
    hi = x.astype(BF16)
    r = x - hi.astype(F32)
    mid = r.astype(BF16)
    lo = (r - mid.astype(F32)).astype(BF16)
    return hi, mid, lo


def _dot_f32_by_01(x, onehot):
    hi, mid, lo = _split3(x)
    return _dot(hi, onehot) + _dot(mid, onehot) + _dot(lo, onehot)


def _dot_f32(a, b):
    a1, a2, a3 = _split3(a)
    b1, b2, b3 = _split3(b)
    small = _dot(a1, b3) + _dot(a3, b1) + _dot(a2, b2)
    mid = _dot(a1, b2) + _dot(a2, b1)
    return _dot(a1, b1) + (mid + small)


def _rms(xf, g):
    ms = jnp.mean(xf * xf, axis=-1, keepdims=True)
    return xf * lax.rsqrt(ms + RMS_EPS) * g


def _norm_kernel(x_ref, g_ref, a_ref):
    a_ref[...] = _rms(x_ref[...], g_ref[...]).astype(a_ref.dtype)


def _norm(x, g):
    n, d = x.shape
    tm = _tile(FFN_ROWS, n)
    return pl.pallas_call(
        _norm_kernel,
        grid=(n // tm,),
        in_specs=[pl.BlockSpec((tm, d), lambda i: (i, 0)),
                  pl.BlockSpec((1, d), lambda i: (0, 0))],
        out_specs=pl.BlockSpec((tm, d), lambda i: (i, 0)),
        out_shape=jax.ShapeDtypeStruct((n, d), BF16),
        compiler_params=_params("parallel"),
        name="rmsnorm",
    )(x, g)


def _proj_kernel(a_ref, w_ref, o_ref, *, sigmoid):
    y = _dot(a_ref[...], w_ref[...])
    if sigmoid:
        y = jax.nn.sigmoid(y)
    o_ref[...] = y.astype(o_ref.dtype)


def _proj(a, w, *, sigmoid, name):
    n, k = a.shape
    p = w.shape[1]
    tm = _tile(PROJ_ROWS, n)
    tn = _tile(PROJ_COLS, p)
    return pl.pallas_call(
        functools.partial(_proj_kernel, sigmoid=sigmoid),
        grid=(n // tm, p // tn),
        in_specs=[pl.BlockSpec((tm, k), lambda i, j: (i, 0)),
                  pl.BlockSpec((k, tn), lambda i, j: (0, j))],
        out_specs=pl.BlockSpec((tm, tn), lambda i, j: (i, j)),
        out_shape=jax.ShapeDtypeStruct((n, p), BF16),
        compiler_params=_params("parallel", "arbitrary"),
        name=name,
    )(a, w)


def _conv_kernel(a_ref, wb_ref, wc_ref, wu_ref, cw_ref, z_ref, carry_ref, *, tiles_per_seq, conv_k):
    i = pl.program_id(0)
    j = pl.program_id(1)

    @pl.when(i % tiles_per_seq == 0)
    def _():
        carry_ref[j] = jnp.zeros(carry_ref.shape[1:], F32)

    a = a_ref[...]
    b = _dot(a, wb_ref[...])
    cu = _dot(a, wc_ref[...]) * _dot(a, wu_ref[...])
    tm = cu.shape[0]
    prev = carry_ref[j]
    row = lax.broadcasted_iota(I32, cu.shape, 0)
    cw = cw_ref[...]
    y = cu * cw[conv_k - 1:conv_k]
    for s in range(1, conv_k):
        sh = pltpu.roll(cu, s, 0)
        for r in range(s):
            src = SUBLANES - s + r
            sh = jnp.where(row == r, prev[src:src + 1], sh)
        y = y + sh * cw[conv_k - 1 - s:conv_k - s]
    z_ref[...] = (b * y).astype(z_ref.dtype)
    carry_ref[j] = cu[tm - SUBLANES:tm]


def _conv_front(a, w_bcu, conv_w_pad, *, d_conv, seq, conv_k):
    n, k = a.shape
    tm = _tile(PROJ_ROWS, seq)
    tn = _tile(CONV_COLS, d_conv)
    nj = d_conv // tn
    return pl.pallas_call(
        functools.partial(_conv_kernel, tiles_per_seq=seq // tm, conv_k=conv_k),
        grid=(n // tm, nj),
        in_specs=[pl.BlockSpec((tm, k), lambda i, j: (i, 0)),
                  pl.BlockSpec((k, tn), lambda i, j: (0, j)),
                  pl.BlockSpec((k, tn), lambda i, j: (0, nj + j)),
                  pl.BlockSpec((k, tn), lambda i, j: (0, 2 * nj + j)),
                  pl.BlockSpec((SUBLANES, tn), lambda i, j: (0, j))],
        out_specs=pl.BlockSpec((tm, tn), lambda i, j: (i, j)),
        out_shape=jax.ShapeDtypeStruct((n, d_conv), BF16),
        scratch_shapes=[pltpu.VMEM((nj, SUBLANES, tn), F32)],
        compiler_params=_params("arbitrary", "arbitrary"),
        name="conv_front",
    )(a, w_bcu, w_bcu, w_bcu, conv_w_pad)


def _cum_kernel(a_ref, wf_ref, bf_ref, c_ref, carry_ref):
    si = pl.program_id(1)

    @pl.when(si == 0)
    def _():
        carry_ref[...] = jnp.zeros(carry_ref.shape, F32)

    lf = jax.nn.log_sigmoid(_dot_nt(wf_ref[...], a_ref[...]) + bf_ref[...])
    tm = lf.shape[1]
    r = lax.broadcasted_iota(I32, (tm, tm), 0)
    c = lax.broadcasted_iota(I32, (tm, tm), 1)
    upper = jnp.where(r <= c, 1.0, 0.0).astype(BF16)
    cs = _dot_f32_by_01(lf, upper) + carry_ref[:, 0:1]
    c_ref[0] = cs * LOG2_E
    carry_ref[...] = jnp.broadcast_to(cs[:, tm - 1:tm], carry_ref.shape)


def _forget_cumsum(a, wf_t, bf_col, *, batch, seq):
    n, k = a.shape
    hp = wf_t.shape[0]
    tm = _tile(CUM_ROWS, seq)
    ns = seq // tm
    return pl.pallas_call(
        _cum_kernel,
        grid=(batch, ns),
        in_specs=[pl.BlockSpec((tm, k), lambda b, s: (b * ns + s, 0)),
                  pl.BlockSpec((hp, k), lambda b, s: (0, 0)),
                  pl.BlockSpec((hp, 1), lambda b, s: (0, 0))],
        out_specs=pl.BlockSpec((1, hp, tm), lambda b, s: (b, 0, s)),
        out_shape=jax.ShapeDtypeStruct((batch, hp, seq), F32),
        scratch_shapes=[pltpu.VMEM((hp, LANES), F32)],
        compiler_params=_params("arbitrary", "arbitrary"),
        name="forget_cumsum",
    )(a, wf_t, bf_col)


def _attn_kernel(q_ref, k_ref, v_ref, c_ref, o_ref, s_ref, p_ref, al_ref, m_ref, acc_ref, *, t):
    qi = pl.program_id(2)
    dh = q_ref.shape[1]
    reps = t // LANES
    q0 = pl.multiple_of(qi * t, t)
    c0 = c_ref[0, 0, :, pl.ds(q0, LANES)][:, 0:1]
    ones_col = jnp.where(lax.broadcasted_iota(I32, (t, LANES), 1) == 0, 1.0, 0.0).astype(BF16)

    def score(slot, k0, masked):
        s = _dot_nt(q_ref[...], k_ref[pl.ds(k0, t), :]) - (c_ref[0, 0, :, pl.ds(k0, t)] - c0)
        if masked:
            row = lax.broadcasted_iota(I32, (t, t), 0)
            col = lax.broadcasted_iota(I32, (t, t), 1)
            s = jnp.where(col <= row, s, -jnp.inf)
        s_ref[slot] = s

    def softmax(slot):
        s = s_ref[slot]
        m_prev = m_ref[...]
        m_new = jnp.maximum(m_prev, jnp.max(s, axis=-1, keepdims=True))
        al_ref[slot] = jnp.exp2(m_prev - m_new)
        p_ref[slot] = jnp.exp2(s - jnp.concatenate([m_new] * reps, axis=1)).astype(BF16)
        m_ref[...] = m_new

    def apply(slot, k0):
        vaug = jnp.concatenate([v_ref[pl.ds(k0, t), :], ones_col], axis=1)
        al = jnp.concatenate([al_ref[slot]] * (acc_ref.shape[1] // LANES), axis=1)
        acc_ref[...] = al * acc_ref[...] + _dot(p_ref[slot], vaug)

    def key_start(item):
        return pl.multiple_of(jnp.where(item == 0, q0, jnp.maximum(item - 1, 0) * t), t)

    m_ref[...] = jnp.full(m_ref.shape, -jnp.inf, F32)
    acc_ref[...] = jnp.zeros(acc_ref.shape, F32)
    p_ref[1] = jnp.zeros(p_ref.shape[1:], BF16)
    al_ref[1] = jnp.ones(al_ref.shape[1:], F32)
    score(0, q0, True)

    def step(i, slot):
        score(1 - slot, pl.multiple_of(i * t, t), False)
        softmax(slot)
        apply(1 - slot, key_start(i - 1))

    def drain(last, slot):
        softmax(slot)
        apply(1 - slot, key_start(last - 1))
        apply(slot, key_start(last))

    def body(g, carry):
        for u in range(ATTN_UNROLL):
            step(g * ATTN_UNROLL + u, u % 2)
        return carry

    last = qi
    groups = last // ATTN_UNROLL
    lax.fori_loop(0, groups, body, 0)
    base = groups * ATTN_UNROLL
    for rem in range(ATTN_UNROLL):
        @pl.when(last - base == rem)
        def _(rem=rem):
            for u in range(rem):
                step(base + u, u % 2)
            drain(last, rem % 2)

    acc = acc_ref[...]
    o_ref[...] = (acc[:, :dh] / acc[:, dh:dh + 1]).astype(o_ref.dtype)


def _attention(qkv, c, *, batch, seq, n_heads, head_dim):
    n = qkv.shape[0]
    t = _tile(ATTN_TILE, seq)
    nq = seq // t
    c = c.reshape(batch, c.shape[1], 1, seq)
    return pl.pallas_call(
        functools.partial(_attn_kernel, t=t),
        grid=(batch, n_heads, nq),
        in_specs=[pl.BlockSpec((t, head_dim), lambda b, h, i: (b * nq + i, h)),
                  pl.BlockSpec((seq, head_dim), lambda b, h, i: (b, n_heads + h)),
                  pl.BlockSpec((seq, head_dim), lambda b, h, i: (b, 2 * n_heads + h)),
                  pl.BlockSpec((1, 1, 1, seq), lambda b, h, i: (b, h, 0, 0))],
        out_specs=pl.BlockSpec((t, head_dim), lambda b, h, i: (b * nq + i, h)),
        out_shape=jax.ShapeDtypeStruct((n, n_heads * head_dim), BF16),
        scratch_shapes=[pltpu.VMEM((2, t, t), F32), pltpu.VMEM((2, t, t), BF16),
                        pltpu.VMEM((2, t, LANES), F32), pltpu.VMEM((t, LANES), F32),
                        pltpu.VMEM((t, head_dim + LANES), F32)],
        compiler_params=_params("parallel", "parallel", "arbitrary"),
        name="forget_attn",
    )(qkv, qkv, qkv, c)


def _mix_kernel(z_ref, o_ref, ga_ref, gb_ref, h_ref, wc_ref, wa_ref, wo_ref, g_ref, hn_ref, a_ref):
    yc = _dot(z_ref[...], wc_ref[...])
    ya = _dot(o_ref[...], wa_ref[...])
    m = ga_ref[...].astype(F32) * yc + gb_ref[...].astype(F32) * ya
    hn = h_ref[...] + _dot(m.astype(BF16), wo_ref[...])
    hn_ref[...] = hn
    a_ref[...] = _rms(hn, g_ref[...]).astype(a_ref.dtype)


def _mix_out(z, o, gates, h, wc, wa, wo, g):
    n, d = h.shape
    dc = z.shape[1]
    da = o.shape[1]
    tm = _tile(MIX_ROWS, n)
    const = lambda i: (0, 0)
    return pl.pallas_call(
        _mix_kernel,
        grid=(n // tm,),
        in_specs=[pl.BlockSpec((tm, dc), lambda i: (i, 0)),
                  pl.BlockSpec((tm, da), lambda i: (i, 0)),
                  pl.BlockSpec((tm, d), lambda i: (i, 0)),
                  pl.BlockSpec((tm, d), lambda i: (i, 1)),
                  pl.BlockSpec((tm, d), lambda i: (i, 0)),
                  pl.BlockSpec((dc, d), const),
                  pl.BlockSpec((da, d), const),
                  pl.BlockSpec((d, d), const),
                  pl.BlockSpec((1, d), const)],
        out_specs=[pl.BlockSpec((tm, d), lambda i: (i, 0)),
                   pl.BlockSpec((tm, d), lambda i: (i, 0))],
        out_shape=[jax.ShapeDtypeStruct((n, d), F32), jax.ShapeDtypeStruct((n, d), BF16)],
        input_output_aliases={4: 0},
        compiler_params=_params("parallel"),
        name="mix_out",
    )(z, o, gates, gates, h, wc, wa, wo, g)


def _epilogue_shapes(n, d, final):
    if final:
        return [jax.ShapeDtypeStruct((n, d), F32)]
    return [jax.ShapeDtypeStruct((n, d), F32), jax.ShapeDtypeStruct((n, d), BF16)]


def _write_epilogue(hn, g, outs, final):
    if final:
        outs[0][...] = _rms(hn, g)
    else:
        outs[0][...] = hn
        outs[1][...] = _rms(hn, g).astype(outs[1].dtype)


def _ffn_kernel(a_ref, wg_ref, wu_ref, wd_ref, h_ref, g_ref, *rest, final):
    outs, acc_ref = rest[:-1], rest[-1]
    f = pl.program_id(1)

    @pl.when(f == 0)
    def _():
        acc_ref[...] = h_ref[...]

    a = a_ref[...]
    hid = (jax.nn.silu(_dot(a, wg_ref[...])) * _dot(a, wu_ref[...])).astype(BF16)
    acc_ref[...] += _dot(hid, wd_ref[...])

    @pl.when(f == pl.num_programs(1) - 1)
    def _():
        _write_epilogue(acc_ref[...], g_ref[...], outs, final)


def _dense_ffn(a, h, wg, wu, wd, g, *, final):
    n, d = h.shape
    ff = wg.shape[1]
    tm = _tile(FFN_ROWS, n)
    tf = _tile(FFN_COLS, ff)
    row = lambda i, f: (i, 0)
    shapes = _epilogue_shapes(n, d, final)
    return pl.pallas_call(
        functools.partial(_ffn_kernel, final=final),
        grid=(n // tm, ff // tf),
        in_specs=[pl.BlockSpec((tm, d), row),
                  pl.BlockSpec((d, tf), lambda i, f: (0, f)),
                  pl.BlockSpec((d, tf), lambda i, f: (0, f)),
                  pl.BlockSpec((tf, d), lambda i, f: (f, 0)),
                  pl.BlockSpec((tm, d), row),
                  pl.BlockSpec((1, d), lambda i, f: (0, 0))],
        out_specs=[pl.BlockSpec((tm, d), row) for _ in shapes],
        out_shape=shapes,
        scratch_shapes=[pltpu.VMEM((tm, d), F32)],
        compiler_params=_params("parallel", "arbitrary"),
        name="dense_ffn",
    )(a, wg, wu, wd, h, g)


def _router_kernel(h_ref, g_ref, rw_ref, rb_ref, rc_ref, rr_ref, cnt_ref, carry_ref, *, n_exp):
    i = pl.program_id(0)

    @pl.when(i == 0)
    def _():
        carry_ref[...] = jnp.zeros(carry_ref.shape, F32)

    a = _rms(h_ref[...], g_ref[...])
    tm = a.shape[0]
    lane = lax.broadcasted_iota(I32, (tm, LANES), 1)
    lanef = lane.astype(F32)
    logits = jnp.where(lane < n_exp, _dot_f32(a, rw_ref[...]) + rb_ref[...], -jnp.inf)
    big = float(LANES)
    m1 = jnp.max(logits, axis=-1, keepdims=True)
    i1 = jnp.min(jnp.where(logits == m1, lanef, big), axis=-1, keepdims=True)
    rest = jnp.where(lanef == i1, -jnp.inf, logits)
    m2 = jnp.max(rest, axis=-1, keepdims=True)
    i2 = jnp.min(jnp.where(rest == m2, lanef, big), axis=-1, keepdims=True)
    e2 = jnp.exp(m2 - m1)
    w1 = 1.0 / (1.0 + e2)
    w2 = e2 / (1.0 + e2)

    onehot = jnp.where(lanef == i1, 1.0, jnp.where(lanef == i2, 1.0, 0.0))
    r = lax.broadcasted_iota(I32, (tm, tm), 0)
    c = lax.broadcasted_iota(I32, (tm, tm), 1)
    strict_lower = jnp.where(c < r, 1.0, 0.0).astype(BF16)
    before = carry_ref[0:1, :]
    prefix = _dot(strict_lower, onehot.astype(BF16)) + before
    rank1 = jnp.sum(jnp.where(lanef == i1, prefix, 0.0), axis=-1, keepdims=True)
    rank2 = jnp.sum(jnp.where(lanef == i2, prefix, 0.0), axis=-1, keepdims=True)
    after = before + jnp.sum(onehot, axis=0, keepdims=True)

    srow = lax.broadcasted_iota(I32, (SUBLANES, LANES), 0)
    cnt_ref[0] = jnp.where(srow == 0, before, jnp.where(srow == 1, after, 0.0))
    carry_ref[...] = jnp.broadcast_to(after, carry_ref.shape)

    rec = jnp.where(lane == 0, i1, jnp.where(lane == 1, i2, jnp.where(
        lane == 2, rank1, jnp.where(lane == 3, rank2, jnp.where(
            lane == 4, w1, jnp.where(lane == 5, w2, 0.0))))))
    rc_ref[...] = rec
    rr_ref[0] = rec.T[0:SUBLANES, :]


def _router(h, g, rw_pad, rb_pad, *, n_exp):
    n, d = h.shape
    tm = _tile(MOE_CHUNK, n)
    nc = n // tm
    return pl.pallas_call(
        functools.partial(_router_kernel, n_exp=n_exp),
        grid=(nc,),
        in_specs=[pl.BlockSpec((tm, d), lambda i: (i, 0)),
                  pl.BlockSpec((1, d), lambda i: (0, 0)),
                  pl.BlockSpec((d, LANES), lambda i: (0, 0)),
                  pl.BlockSpec((1, LANES), lambda i: (0, 0))],
        out_specs=[pl.BlockSpec((tm, LANES), lambda i: (i, 0)),
                   pl.BlockSpec((1, SUBLANES, tm), lambda i: (i, 0, 0)),
                   pl.BlockSpec((1, SUBLANES, LANES), lambda i: (i, 0, 0))],
        out_shape=[jax.ShapeDtypeStruct((n, LANES), F32),
                   jax.ShapeDtypeStruct((nc, SUBLANES, tm), F32),
                   jax.ShapeDtypeStruct((nc, SUBLANES, LANES), F32)],
        scratch_shapes=[pltpu.VMEM((SUBLANES, LANES), F32)],
        compiler_params=_params("arbitrary"),
        name="moe_router",
    )(h, g, rw_pad, rb_pad)


def _moe_plan(cnt, *, n_exp, n_tiles, n_items):
    tile, rb, ch = MOE_TILE, MOE_ROWBLK, MOE_CHUNK
    cb = cnt[:, 0, :n_exp].astype(I32)
    ca = cnt[:, 1, :n_exp].astype(I32)
    nc = cb.shape[0]
    counts = ca[-1]
    tiles_e = (counts + tile - 1) // tile
    tile_end = jnp.cumsum(tiles_e)
    n_used = tile_end[-1]
    pad_start = (tile_end - tiles_e) * tile
    t_ids = jnp.arange(n_tiles, dtype=I32)
    tile_e = jnp.minimum(jnp.searchsorted(tile_end, t_ids, side="right"), n_exp - 1).astype(I32)
    tile_e = jnp.where(t_ids < n_used, tile_e, tile_e[n_used - 1])

    w_ids = jnp.arange(n_items, dtype=I32)

    per_tile = tile // rb
    nrb = n_tiles * per_tile
    rb_ids = jnp.arange(nrb, dtype=I32)
    rb_e = tile_e[rb_ids // per_tile]
    rb_used = (rb_ids // per_tile) < n_used
    k0 = rb_ids * rb - pad_start[rb_e]
    k1 = jnp.minimum(k0 + rb, counts[rb_e])
    has = rb_used & (k0 < counts[rb_e])
    c_lo = jnp.sum(ca.T[rb_e] <= k0[:, None], axis=1).astype(I32)
    c_hi = jnp.sum(cb.T[rb_e] < k1[:, None], axis=1).astype(I32) - 1
    n_it = jnp.where(has, c_hi - c_lo + 1, 1).astype(I32)
    c_lo = jnp.where(has, c_lo, 0)
    off_end = jnp.cumsum(n_it)
    off_start = off_end - n_it
    total = off_end[-1]
    g_rb = jnp.minimum(jnp.searchsorted(off_end, w_ids, side="right"), nrb - 1).astype(I32)
    g_c = c_lo[g_rb] + w_ids - off_start[g_rb]
    g_valid = w_ids < total
    g_first = g_valid & (w_ids == off_start[g_rb])
    g_last = g_valid & (w_ids == off_end[g_rb] - 1)
    g_rb = jnp.where(g_valid, g_rb, g_rb[total - 1])
    g_c = jnp.where(g_valid, g_c, g_c[total - 1])
    g_fl = g_first.astype(I32) + 2 * g_last.astype(I32) + 4 * g_valid.astype(I32)

    span = ch // rb + 1
    row_lo = pad_start[None, :] + cb
    row_hi = pad_start[None, :] + ca - 1
    nblk = jnp.where(ca > cb, row_hi // rb - row_lo // rb + 1, 0)
    ks = jnp.arange(span, dtype=I32)
    cand_rb = ((row_lo // rb)[..., None] + ks).reshape(-1)
    cand_ok = (ks < nblk[..., None]).reshape(-1)
    idx = jnp.nonzero(cand_ok, size=n_items, fill_value=0)[0].astype(I32)
    total2 = jnp.sum(cand_ok).astype(I32)
    s_valid = w_ids < total2
    s_c = idx // (n_exp * span)
    s_rb = cand_rb[idx]
    s_c = jnp.where(s_valid, s_c, s_c[total2 - 1])
    s_rb = jnp.where(s_valid, s_rb, s_rb[total2 - 1])
    prev_c = jnp.concatenate([jnp.full((1,), -1, I32), s_c[:-1]])
    next_c = jnp.concatenate([s_c[1:], jnp.full((1,), -1, I32)])
    s_first = s_valid & (s_c != prev_c)
    s_last = s_valid & ((s_c != next_c) | (w_ids == total2 - 1))
    s_fl = s_first.astype(I32) + 2 * s_last.astype(I32) + 4 * s_valid.astype(I32)

    return dict(pad_start=pad_start.astype(I32), tile_e=tile_e, n_used=n_used.reshape(1).astype(I32),
                g_rb=g_rb, g_c=g_c, g_fl=g_fl, s_c=s_c.astype(I32), s_rb=s_rb.astype(I32), s_fl=s_fl)


def _dest(e, rank, ps_ref, n_exp):
    base = jnp.zeros_like(rank)
    for x in range(n_exp):
        base = jnp.where(e == float(x), ps_ref[x].astype(F32), base)
    return base + rank


def _gather_kernel(rb_ref, c_ref, fl_ref, ps_ref, rr_ref, a_ref, xb_ref, rw_ref, acc_ref, accw_ref, *, n_exp):
    w = pl.program_id(0)
    fl = fl_ref[w]
    nrow, ch = acc_ref.shape[0], a_ref.shape[0]

    @pl.when((fl & 1) != 0)
    def _():
        acc_ref[...] = jnp.zeros(acc_ref.shape, F32)
        accw_ref[...] = jnp.zeros(accw_ref.shape, F32)

    @pl.when((fl & 4) != 0)
    def _():
        rr = rr_ref[0]
        row0 = (rb_ref[w] * nrow).astype(F32)
        d1 = _dest(rr[0:1], rr[2:3], ps_ref, n_exp) - row0
        d2 = _dest(rr[1:2], rr[3:4], ps_ref, n_exp) - row0
        j = lax.broadcasted_iota(I32, (nrow, ch), 0).astype(F32)
        hit1 = d1 == j
        hit2 = d2 == j
        sel = jnp.where(hit1, 1.0, jnp.where(hit2, 1.0, 0.0)).astype(BF16)
        acc_ref[...] += _dot(sel, a_ref[...])
        wsel = jnp.where(hit1, rr[4:5], jnp.where(hit2, rr[5:6], 0.0))
        accw_ref[...] += jnp.sum(wsel, axis=-1, keepdims=True)

    @pl.when((fl & 2) != 0)
    def _():
        xb_ref[...] = acc_ref[...].astype(xb_ref.dtype)
        rw_ref[...] = accw_ref[...]


def _moe_gather(plan, rr, a, *, n_exp, n_rows):
    n, d = a.shape
    ch, rb = rr.shape[2], MOE_ROWBLK
    n_items = plan["g_rb"].shape[0]
    spec = pltpu.PrefetchScalarGridSpec(
        num_scalar_prefetch=4,
        grid=(n_items,),
        in_specs=[pl.BlockSpec((1, SUBLANES, ch), lambda w, rbi, ci, fl, ps: (ci[w], 0, 0)),
                  pl.BlockSpec((ch, d), lambda w, rbi, ci, fl, ps: (ci[w], 0))],
        out_specs=[pl.BlockSpec((rb, d), lambda w, rbi, ci, fl, ps: (rbi[w], 0)),
                   pl.BlockSpec((rb, LANES), lambda w, rbi, ci, fl, ps: (rbi[w], 0))],
        scratch_shapes=[pltpu.VMEM((rb, d), F32), pltpu.VMEM((rb, LANES), F32)],
    )
    return pl.pallas_call(
        functools.partial(_gather_kernel, n_exp=n_exp),
        grid_spec=spec,
        out_shape=[jax.ShapeDtypeStruct((n_rows, d), BF16),
                   jax.ShapeDtypeStruct((n_rows, LANES), F32)],
        compiler_params=_params("arbitrary"),
        name="moe_gather",
    )(plan["g_rb"], plan["g_c"], plan["g_fl"], plan["pad_start"], rr, a)


def _expert_kernel(te_ref, nu_ref, x_ref, wg_ref, wu_ref, wd_ref, rw_ref, y_ref, acc_ref):
    t = pl.program_id(0)
    f = pl.program_id(1)

    @pl.when(t < nu_ref[0])
    def _():
        @pl.when(f == 0)
        def _():
            acc_ref[...] = jnp.zeros(acc_ref.shape, F32)

        x = x_ref[...]
        hid = (jax.nn.silu(_dot(x, wg_ref[0])) * _dot(x, wu_ref[0])).astype(BF16)
        acc_ref[...] += _dot(hid, wd_ref[0])

        @pl.when(f == pl.num_programs(1) - 1)
        def _():
            y_ref[...] = (acc_ref[...] * rw_ref[:, 0:1]).astype(y_ref.dtype)

    @pl.when((t >= nu_ref[0]) & (f == pl.num_programs(1) - 1))
    def _():
        y_ref[...] = jnp.zeros(y_ref.shape, y_ref.dtype)


def _moe_experts(plan, xb, roww, wg, wu, wd):
    n_rows, d = xb.shape
    ff = wg.shape[2]
    tile = MOE_TILE
    tf = _tile(MOE_COLS, ff)
    nf = ff // tf
    n_tiles = n_rows // tile

    def row(t, f, te, nu):
        return (jnp.minimum(t, nu[0] - 1), 0)

    def fcol(t, f, nu):
        return jnp.where(t < nu[0], f, nf - 1)

    spec = pltpu.PrefetchScalarGridSpec(
        num_scalar_prefetch=2,
        grid=(n_tiles, nf),
        in_specs=[pl.BlockSpec((tile, d), row),
                  pl.BlockSpec((1, d, tf), lambda t, f, te, nu: (te[t], 0, fcol(t, f, nu))),
                  pl.BlockSpec((1, d, tf), lambda t, f, te, nu: (te[t], 0, fcol(t, f, nu))),
                  pl.BlockSpec((1, tf, d), lambda t, f, te, nu: (te[t], fcol(t, f, nu), 0)),
                  pl.BlockSpec((tile, LANES), row)],
        out_specs=pl.BlockSpec((tile, d), lambda t, f, te, nu: (t, 0)),
        scratch_shapes=[pltpu.VMEM((tile, d), F32)],
    )
    return pl.pallas_call(
        _expert_kernel,
        grid_spec=spec,
        out_shape=jax.ShapeDtypeStruct((n_rows, d), BF16),
        compiler_params=_params("arbitrary", "arbitrary"),
        name="moe_experts",
    )(plan["tile_e"], plan["n_used"], xb, wg, wu, wd, roww)


def _combine_kernel(c_ref, rb_ref, fl_ref, ps_ref, rc_ref, y_ref, h_ref, g_ref, *rest, n_exp, final):
    outs, acc_ref = rest[:-1], rest[-1]
    w = pl.program_id(0)
    fl = fl_ref[w]
    ch, nrow = acc_ref.shape[0], y_ref.shape[0]

    @pl.when((fl & 1) != 0)
    def _():
        acc_ref[...] = h_ref[...]

    @pl.when((fl & 4) != 0)
    def _():
        rc = rc_ref[...]
        row0 = (rb_ref[w] * nrow).astype(F32)
        d1 = _dest(rc[:, 0:1], rc[:, 2:3], ps_ref, n_exp) - row0
        d2 = _dest(rc[:, 1:2], rc[:, 3:4], ps_ref, n_exp) - row0
        j = lax.broadcasted_iota(I32, (ch, nrow), 1).astype(F32)
        sel = jnp.where(d1 == j, 1.0, jnp.where(d2 == j, 1.0, 0.0)).astype(BF16)
        acc_ref[...] += _dot(sel, y_ref[...])

    @pl.when((fl & 2) != 0)
    def _():
        _write_epilogue(acc_ref[...], g_ref[...], outs, final)


def _moe_combine(plan, rc, yb, h, g, *, n_exp, final):
    n, d = h.shape
    ch, rb = MOE_CHUNK if n >= MOE_CHUNK else n, MOE_ROWBLK
    n_items = plan["s_c"].shape[0]
    tok = lambda w, ci, rbi, fl, ps: (ci[w], 0)
    shapes = _epilogue_shapes(n, d, final)
    spec = pltpu.PrefetchScalarGridSpec(
        num_scalar_prefetch=4,
        grid=(n_items,),
        in_specs=[pl.BlockSpec((ch, LANES), tok),
                  pl.BlockSpec((rb, d), lambda w, ci, rbi, fl, ps: (rbi[w], 0)),
                  pl.BlockSpec((ch, d), tok),
                  pl.BlockSpec((1, d), lambda w, ci, rbi, fl, ps: (0, 0))],
        out_specs=[pl.BlockSpec((ch, d), tok) for _ in shapes],
        scratch_shapes=[pltpu.VMEM((ch, d), F32)],
    )
    return pl.pallas_call(
        functools.partial(_combine_kernel, n_exp=n_exp, final=final),
        grid_spec=spec,
        out_shape=shapes,
        compiler_params=_params("arbitrary"),
        name="moe_combine",
    )(plan["s_c"], plan["s_rb"], plan["s_fl"], plan["pad_start"], rc, yb, h, g)


def _moe_ffn(a, h, g_ffn, g_next, rw, rb, wg, wu, wd, *, final):
    n, d = h.shape
    n_exp = rw.shape[1]
    assert n_exp <= LANES and TOP_K == 2
    ch = _tile(MOE_CHUNK, n)
    assert ch == MOE_CHUNK and MOE_TILE % MOE_ROWBLK == 0
    rw_pad = jnp.zeros((d, LANES), F32).at[:, :n_exp].set(rw)
    rb_pad = jnp.zeros((1, LANES), F32).at[0, :n_exp].set(rb)
    rc, rr, cnt = _router(h, g_ffn, rw_pad, rb_pad, n_exp=n_exp)
    n_tiles = (n * TOP_K) // MOE_TILE + n_exp
    n_items = n_tiles * (MOE_TILE // MOE_ROWBLK) + n_exp * (n // ch)
    plan = _moe_plan(cnt, n_exp=n_exp, n_tiles=n_tiles, n_items=n_items)
    xb, roww = _moe_gather(plan, rr, a, n_exp=n_exp, n_rows=n_tiles * MOE_TILE)
    yb = _moe_experts(plan, xb, roww, wg, wu, wd)
    return _moe_combine(plan, rc, yb, h, g_next, n_exp=n_exp, final=final)


def kernel(x, mix_norm, w_in, b_forget, conv_w, w_conv_out, w_attn_out, w_o, ffn_norm, dense_w_gate, dense_w_up, dense_w_down, router_w, router_b, moe_w_gate, moe_w_up, moe_w_down, final_norm):
    batch, seq, d = x.shape
    depth = mix_norm.shape[0]
    conv_k, d_conv = conv_w.shape[1], conv_w.shape[2]
    d_attn = w_attn_out.shape[1]
    n_heads = b_forget.shape[1]
    head_dim = d_attn // n_heads
    assert head_dim % LANES == 0 and conv_k - 1 <= SUBLANES
    n = batch * seq
    hp = 2 * SUBLANES
    assert n_heads <= hp
    scale = LOG2_E / math.sqrt(head_dim)
    o_q = 3 * d_conv
    o_f = o_q + 3 * d_attn
    o_g = o_f + n_heads

    h = x.reshape(n, d)
    a = _norm(h, mix_norm[0].reshape(1, d))
    out = None
    for layer in range(depth):
        wl = w_in[layer]
        w_bcu = wl[:, :o_q].astype(BF16)
        w_qkv = jnp.concatenate([wl[:, o_q:o_q + d_attn] * scale, wl[:, o_q + d_attn:o_f]], axis=1).astype(BF16)
        wf_t = jnp.zeros((hp, d), F32).at[:n_heads].set(wl[:, o_f:o_g].T).astype(BF16)
        bf_col = jnp.zeros((hp, 1), F32).at[:n_heads, 0].set(b_forget[layer])
        w_gates = wl[:, o_g:].astype(BF16)
        cw_pad = jnp.zeros((SUBLANES, d_conv), F32).at[:conv_k].set(conv_w[layer])

        z = _conv_front(a, w_bcu, cw_pad, d_conv=d_conv, seq=seq, conv_k=conv_k)
        qkv = _proj(a, w_qkv, sigmoid=False, name="qkv_proj")
        gates = _proj(a, w_gates, sigmoid=True, name="gate_proj")
        c = _forget_cumsum(a, wf_t, bf_col, batch=batch, seq=seq)
        o = _attention(qkv, c, batch=batch, seq=seq, n_heads=n_heads, head_dim=head_dim)
        h, a = _mix_out(z, o, gates, h, w_conv_out[layer].astype(BF16), w_attn_out[layer].astype(BF16),
                        w_o[layer].astype(BF16), ffn_norm[layer].reshape(1, d))

        final = layer == depth - 1
        g_next = (final_norm if final else mix_norm[layer + 1]).reshape(1, d)
        i = layer // 2
        if layer % 2 == 0:
            res = _dense_ffn(a, h, dense_w_gate[i].astype(BF16), dense_w_up[i].astype(BF16),
                             dense_w_down[i].astype(BF16), g_next, final=final)
        else:
            res = _moe_ffn(a, h, ffn_norm[layer].reshape(1, d), g_next, router_w[i], router_b[i],
                           moe_w_gate[i].astype(BF16), moe_w_up[i].astype(BF16),
                           moe_w_down[i].astype(BF16), final=final)
        if final:
            out = res[0]
        else:
            h, a = res
    return out.reshape(batch, seq, d)
```

```python
import functools
import math

import jax
import jax.numpy as jnp
from jax import lax
from jax.experimental import pallas as pl
from jax.experimental.pallas import tpu as pltpu

F32 = jnp.float32
BF16 = jnp.bfloat16
I32 = jnp.int32
U32 = jnp.uint32

RMS_EPS = 1e-6
LOG2_E = math.log2(math.e)
TOP_K = 2
LANES = 128
SUBLANES = 8
VMEM_LIMIT = 60 * 1024 * 1024

PROJ_ROWS = 1024
PROJ_COLS = 1024
CONV_COLS = 512
CUM_ROWS = 512
ATTN_TILE = 512
ATTN_UNROLL = 4
MIX_ROWS = 256
FFN_ROWS = 512
FFN_COLS = 512
MOE_CHUNK = 512
MOE_TILE = 1024
MOE_COLS = 1024


def _tile(pref, dim):
    t = min(pref, dim)
    assert dim % t == 0, (pref, dim)
    return t


def _params(*sem):
    return pltpu.CompilerParams(dimension_semantics=sem, vmem_limit_bytes=VMEM_LIMIT)


def _dot(a, b):
    return jnp.dot(a, b, preferred_element_type=F32)


def _dot_nt(a, b):
    return lax.dot_general(a, b, (((1,), (1,)), ((), ())), preferred_element_type=F32)


def _split3(x):
    hi = x.astype(BF16)
    r = x - hi.astype(F32)
    mid = r.astype(BF16)
    lo = (r - mid.astype(F32)).astype(BF16)
    return hi, mid, lo


def _dot_f32_by_01(x, onehot):
    hi, mid, lo = _split3(x)
    return _dot(hi, onehot) + _dot(mid, onehot) + _dot(lo, onehot)


def _dot_f32(a, b):
    a1, a2, a3 = _split3(a)
    b1, b2, b3 = _split3(b)
    small = _dot(a1, b3) + _dot(a3, b1) + _dot(a2, b2)
    mid = _dot(a1, b2) + _dot(a2, b1)
    return _dot(a1, b1) + (mid + small)


def _rms(xf, g):
    ms = jnp.mean(xf * xf, axis=-1, keepdims=True)
    return xf * lax.rsqrt(ms + RMS_EPS) * g


def _pack_rows(ref, x):
    m, d = x.shape
    half = d // 2
    assert half == SUBLANES * LANES
    lo = lax.bitcast_convert_type(x[:, :half].astype(BF16).astype(F32), U32) >> 16
    hi = lax.bitcast_convert_type(x[:, half:].astype(BF16).astype(F32), U32) & jnp.uint32(0xFFFF0000)
    w = hi | lo
    for r in range(SUBLANES):
        ref[pl.ds(r, m, stride=SUBLANES), :] = w[:, r * LANES:(r + 1) * LANES]


def _unpack_rows(ref, m):
    w = jnp.concatenate([ref[pl.ds(r, m, stride=SUBLANES), :] for r in range(SUBLANES)], axis=1)
    lo = lax.bitcast_convert_type(w << 16, F32)
    hi = lax.bitcast_convert_type(w & jnp.uint32(0xFFFF0000), F32)
    return lo, hi


def _norm_kernel(x_ref, g_ref, a_ref):
    a_ref[...] = _rms(x_ref[...], g_ref[...]).astype(a_ref.dtype)


def _norm(x, g):
    n, d = x.shape
    tm = _tile(FFN_ROWS, n)
    return pl.pallas_call(
        _norm_kernel,
        grid=(n // tm,),
        in_specs=[pl.BlockSpec((tm, d), lambda i: (i, 0)),
                  pl.BlockSpec((1, d), lambda i: (0, 0))],
        out_specs=pl.BlockSpec((tm, d), lambda i: (i, 0)),
        out_shape=jax.ShapeDtypeStruct((n, d), BF16),
        compiler_params=_params("parallel"),
        name="rmsnorm",
    )(x, g)


def _proj_kernel(a_ref, w_ref, o_ref, *, sigmoid):
    y = _dot(a_ref[...], w_ref[...])
    if sigmoid:
        y = jax.nn.sigmoid(y)
    o_ref[...] = y.astype(o_ref.dtype)


def _proj(a, w, *, sigmoid, name):
    n, k = a.shape
    p = w.shape[1]
    tm = _tile(PROJ_ROWS, n)
    tn = _tile(PROJ_COLS, p)
    return pl.pallas_call(
        functools.partial(_proj_kernel, sigmoid=sigmoid),
        grid=(n // tm, p // tn),
        in_specs=[pl.BlockSpec((tm, k), lambda i, j: (i, 0)),
                  pl.BlockSpec((k, tn), lambda i, j: (0, j))],
        out_specs=pl.BlockSpec((tm, tn), lambda i, j: (i, j)),
        out_shape=jax.ShapeDtypeStruct((n, p), BF16),
        compiler_params=_params("parallel", "arbitrary"),
        name=name,
    )(a, w)


def _conv_kernel(a_ref, wb_ref, wc_ref, wu_ref, cw_ref, z_ref, carry_ref, *, tiles_per_seq, conv_k):
    i = pl.program_id(0)
    j = pl.program_id(1)

    @pl.when(i % tiles_per_seq == 0)
    def _():
        carry_ref[j] = jnp.zeros(carry_ref.shape[1:], F32)

    a = a_ref[...]
    b = _dot(a, wb_ref[...])
    cu = _dot(a, wc_ref[...]) * _dot(a, wu_ref[...])
    tm = cu.shape[0]
    prev = carry_ref[j]
    row = lax.broadcasted_iota(I32, cu.shape, 0)
    cw = cw_ref[...]
    y = cu * cw[conv_k - 1:conv_k]
    for s in range(1, conv_k):
        sh = pltpu.roll(cu, s, 0)
        for r in range(s):
            src = SUBLANES - s + r
            sh = jnp.where(row == r, prev[src:src + 1], sh)
        y = y + sh * cw[conv_k - 1 - s:conv_k - s]
    z_ref[...] = (b * y).astype(z_ref.dtype)
    carry_ref[j] = cu[tm - SUBLANES:tm]


def _conv_front(a, w_bcu, conv_w_pad, *, d_conv, seq, conv_k):
    n, k = a.shape
    tm = _tile(PROJ_ROWS, seq)
    tn = _tile(CONV_COLS, d_conv)
    nj = d_conv // tn
    return pl.pallas_call(
        functools.partial(_conv_kernel, tiles_per_seq=seq // tm, conv_k=conv_k),
        grid=(n // tm, nj),
        in_specs=[pl.BlockSpec((tm, k), lambda i, j: (i, 0)),
                  pl.BlockSpec((k, tn), lambda i, j: (0, j)),
                  pl.BlockSpec((k, tn), lambda i, j: (0, nj + j)),
                  pl.BlockSpec((k, tn), lambda i, j: (0, 2 * nj + j)),
                  pl.BlockSpec((SUBLANES, tn), lambda i, j: (0, j))],
        out_specs=pl.BlockSpec((tm, tn), lambda i, j: (i, j)),
        out_shape=jax.ShapeDtypeStruct((n, d_conv), BF16),
        scratch_shapes=[pltpu.VMEM((nj, SUBLANES, tn), F32)],
        compiler_params=_params("arbitrary", "arbitrary"),
        name="conv_front",
    )(a, w_bcu, w_bcu, w_bcu, conv_w_pad)


def _cum_kernel(a_ref, wf_ref, bf_ref, c_ref, carry_ref):
    si = pl.program_id(1)

    @pl.when(si == 0)
    def _():
        carry_ref[...] = jnp.zeros(carry_ref.shape, F32)

    lf = jax.nn.log_sigmoid(_dot_nt(wf_ref[...], a_ref[...]) + bf_ref[...])
    tm = lf.shape[1]
    r = lax.broadcasted_iota(I32, (tm, tm), 0)
    c = lax.broadcasted_iota(I32, (tm, tm), 1)
    upper = jnp.where(r <= c, 1.0, 0.0).astype(BF16)
    cs = _dot_f32_by_01(lf, upper) + carry_ref[:, 0:1]
    c_ref[0] = cs * LOG2_E
    carry_ref[...] = jnp.broadcast_to(cs[:, tm - 1:tm], carry_ref.shape)


def _forget_cumsum(a, wf_t, bf_col, *, batch, seq):
    n, k = a.shape
    hp = wf_t.shape[0]
    tm = _tile(CUM_ROWS, seq)
    ns = seq // tm
    return pl.pallas_call(
        _cum_kernel,
        grid=(batch, ns),
        in_specs=[pl.BlockSpec((tm, k), lambda b, s: (b * ns + s, 0)),
                  pl.BlockSpec((hp, k), lambda b, s: (0, 0)),
                  pl.BlockSpec((hp, 1), lambda b, s: (0, 0))],
        out_specs=pl.BlockSpec((1, hp, tm), lambda b, s: (b, 0, s)),
        out_shape=jax.ShapeDtypeStruct((batch, hp, seq), F32),
        scratch_shapes=[pltpu.VMEM((hp, LANES), F32)],
        compiler_params=_params("arbitrary", "arbitrary"),
        name="forget_cumsum",
    )(a, wf_t, bf_col)


def _attn_kernel(q_ref, k_ref, v_ref, c_ref, o_ref, s_ref, p_ref, al_ref, m_ref, acc_ref, *, t):
    qi = pl.program_id(2)
    dh = q_ref.shape[1]
    reps = t // LANES
    q0 = pl.multiple_of(qi * t, t)
    c0 = c_ref[0, 0, :, pl.ds(q0, LANES)][:, 0:1]
    ones_col = jnp.where(lax.broadcasted_iota(I32, (t, LANES), 1) == 0, 1.0, 0.0).astype(BF16)

    def score(slot, k0, masked):
        s = _dot_nt(q_ref[...], k_ref[pl.ds(k0, t), :]) - (c_ref[0, 0, :, pl.ds(k0, t)] - c0)
        if masked:
            row = lax.broadcasted_iota(I32, (t, t), 0)
            col = lax.broadcasted_iota(I32, (t, t), 1)
            s = jnp.where(col <= row, s, -jnp.inf)
        s_ref[slot] = s

    def softmax(slot):
        s = s_ref[slot]
        m_prev = m_ref[...]
        m_new = jnp.maximum(m_prev, jnp.max(s, axis=-1, keepdims=True))
        al_ref[slot] = jnp.exp2(m_prev - m_new)
        p_ref[slot] = jnp.exp2(s - jnp.concatenate([m_new] * reps, axis=1)).astype(BF16)
        m_ref[...] = m_new

    def apply(slot, k0):
        vaug = jnp.concatenate([v_ref[pl.ds(k0, t), :], ones_col], axis=1)
        al = jnp.concatenate([al_ref[slot]] * (acc_ref.shape[1] // LANES), axis=1)
        acc_ref[...] = al * acc_ref[...] + _dot(p_ref[slot], vaug)

    def key_start(item):
        return pl.multiple_of(jnp.where(item == 0, q0, jnp.maximum(item - 1, 0) * t), t)

    m_ref[...] = jnp.full(m_ref.shape, -jnp.inf, F32)
    acc_ref[...] = jnp.zeros(acc_ref.shape, F32)
    p_ref[1] = jnp.zeros(p_ref.shape[1:], BF16)
    al_ref[1] = jnp.ones(al_ref.shape[1:], F32)
    score(0, q0, True)

    def step(i, slot):
        score(1 - slot, pl.multiple_of(i * t, t), False)
        softmax(slot)
        apply(1 - slot, key_start(i - 1))

    def drain(last, slot):
        softmax(slot)
        apply(1 - slot, key_start(last - 1))
        apply(slot, key_start(last))

    def body(g, carry):
        for u in range(ATTN_UNROLL):
            step(g * ATTN_UNROLL + u, u % 2)
        return carry

    last = qi
    groups = last // ATTN_UNROLL
    lax.fori_loop(0, groups, body, 0)
    base = groups * ATTN_UNROLL
    for rem in range(ATTN_UNROLL):
        @pl.when(last - base == rem)
        def _(rem=rem):
            for u in range(rem):
                step(base + u, u % 2)
            drain(last, rem % 2)

    acc = acc_ref[...]
    o_ref[...] = (acc[:, :dh] / acc[:, dh:dh + 1]).astype(o_ref.dtype)


def _attention(qkv, c, *, batch, seq, n_heads, head_dim):
    n = qkv.shape[0]
    t = _tile(ATTN_TILE, seq)
    nq = seq // t
    c = c.reshape(batch, c.shape[1], 1, seq)
    return pl.pallas_call(
        functools.partial(_attn_kernel, t=t),
        grid=(batch, n_heads, nq),
        in_specs=[pl.BlockSpec((t, head_dim), lambda b, h, i: (b * nq + i, h)),
                  pl.BlockSpec((seq, head_dim), lambda b, h, i: (b, n_heads + h)),
                  pl.BlockSpec((seq, head_dim), lambda b, h, i: (b, 2 * n_heads + h)),
                  pl.BlockSpec((1, 1, 1, seq), lambda b, h, i: (b, h, 0, 0))],
        out_specs=pl.BlockSpec((t, head_dim), lambda b, h, i: (b * nq + i, h)),
        out_shape=jax.ShapeDtypeStruct((n, n_heads * head_dim), BF16),
        scratch_shapes=[pltpu.VMEM((2, t, t), F32), pltpu.VMEM((2, t, t), BF16),
                        pltpu.VMEM((2, t, LANES), F32), pltpu.VMEM((t, LANES), F32),
                        pltpu.VMEM((t, head_dim + LANES), F32)],
        compiler_params=_params("parallel", "parallel", "arbitrary"),
        name="forget_attn",
    )(qkv, qkv, qkv, c)


def _mix_kernel(z_ref, o_ref, ga_ref, gb_ref, h_ref, wc_ref, wa_ref, wo_ref, g_ref, hn_ref, a_ref, *, packed):
    yc = _dot(z_ref[...], wc_ref[...])
    ya = _dot(o_ref[...], wa_ref[...])
    m = ga_ref[...].astype(F32) * yc + gb_ref[...].astype(F32) * ya
    hn = h_ref[...] + _dot(m.astype(BF16), wo_ref[...])
    hn_ref[...] = hn
    a = _rms(hn, g_ref[...])
    if packed:
        _pack_rows(a_ref, a)
    else:
        a_ref[...] = a.astype(a_ref.dtype)


def _mix_out(z, o, gates, h, wc, wa, wo, g, *, packed):
    n, d = h.shape
    dc = z.shape[1]
    da = o.shape[1]
    tm = _tile(MIX_ROWS, n)
    const = lambda i: (0, 0)
    if packed:
        a_spec = pl.BlockSpec((tm * SUBLANES, LANES), lambda i: (i, 0))
        a_shape = jax.ShapeDtypeStruct((n * SUBLANES, LANES), U32)
    else:
        a_spec = pl.BlockSpec((tm, d), lambda i: (i, 0))
        a_shape = jax.ShapeDtypeStruct((n, d), BF16)
    return pl.pallas_call(
        functools.partial(_mix_kernel, packed=packed),
        grid=(n // tm,),
        in_specs=[pl.BlockSpec((tm, dc), lambda i: (i, 0)),
                  pl.BlockSpec((tm, da), lambda i: (i, 0)),
                  pl.BlockSpec((tm, d), lambda i: (i, 0)),
                  pl.BlockSpec((tm, d), lambda i: (i, 1)),
                  pl.BlockSpec((tm, d), lambda i: (i, 0)),
                  pl.BlockSpec((dc, d), const),
                  pl.BlockSpec((da, d), const),
                  pl.BlockSpec((d, d), const),
                  pl.BlockSpec((1, d), const)],
        out_specs=[pl.BlockSpec((tm, d), lambda i: (i, 0)), a_spec],
        out_shape=[jax.ShapeDtypeStruct((n, d), F32), a_shape],
        input_output_aliases={4: 0},
        compiler_params=_params("parallel"),
        name="mix_out",
    )(z, o, gates, gates, h, wc, wa, wo, g)


def _epilogue_shapes(n, d, final):
    if final:
        return [jax.ShapeDtypeStruct((n, d), F32)]
    return [jax.ShapeDtypeStruct((n, d), F32), jax.ShapeDtypeStruct((n, d), BF16)]


def _write_epilogue(hn, g, outs, final):
    if final:
        outs[0][...] = _rms(hn, g)
    else:
        outs[0][...] = hn
        outs[1][...] = _rms(hn, g).astype(outs[1].dtype)


def _ffn_kernel(a_ref, wg_ref, wu_ref, wd_ref, h_ref, g_ref, *rest, final):
    outs, acc_ref = rest[:-1], rest[-1]
    f = pl.program_id(1)

    @pl.when(f == 0)
    def _():
        acc_ref[...] = h_ref[...]

    a = a_ref[...]
    hid = (jax.nn.silu(_dot(a, wg_ref[...])) * _dot(a, wu_ref[...])).astype(BF16)
    acc_ref[...] += _dot(hid, wd_ref[...])

    @pl.when(f == pl.num_programs(1) - 1)
    def _():
        _write_epilogue(acc_ref[...], g_ref[...], outs, final)


def _dense_ffn(a, h, wg, wu, wd, g, *, final):
    n, d = h.shape
    ff = wg.shape[1]
    tm = _tile(FFN_ROWS, n)
    tf = _tile(FFN_COLS, ff)
    row = lambda i, f: (i, 0)
    shapes = _epilogue_shapes(n, d, final)
    return pl.pallas_call(
        functools.partial(_ffn_kernel, final=final),
        grid=(n // tm, ff // tf),
        in_specs=[pl.BlockSpec((tm, d), row),
                  pl.BlockSpec((d, tf), lambda i, f: (0, f)),
                  pl.BlockSpec((d, tf), lambda i, f: (0, f)),
                  pl.BlockSpec((tf, d), lambda i, f: (f, 0)),
                  pl.BlockSpec((tm, d), row),
                  pl.BlockSpec((1, d), lambda i, f: (0, 0))],
        out_specs=[pl.BlockSpec((tm, d), row) for _ in shapes],
        out_shape=shapes,
        scratch_shapes=[pltpu.VMEM((tm, d), F32)],
        compiler_params=_params("parallel", "arbitrary"),
        name="dense_ffn",
    )(a, wg, wu, wd, h, g)


def _router_kernel(h_ref, g_ref, rw_ref, rb_ref, rc_ref, cnt_ref, carry_ref, *, n_exp):
    i = pl.program_id(0)

    @pl.when(i == 0)
    def _():
        carry_ref[...] = jnp.zeros(carry_ref.shape, F32)

    a = _rms(h_ref[...], g_ref[...])
    tm = a.shape[0]
    lane = lax.broadcasted_iota(I32, (tm, LANES), 1)
    lanef = lane.astype(F32)
    logits = jnp.where(lane < n_exp, _dot_f32(a, rw_ref[...]) + rb_ref[...], -jnp.inf)
    big = float(LANES)
    m1 = jnp.max(logits, axis=-1, keepdims=True)
    i1 = jnp.min(jnp.where(logits == m1, lanef, big), axis=-1, keepdims=True)
    rest = jnp.where(lanef == i1, -jnp.inf, logits)
    m2 = jnp.max(rest, axis=-1, keepdims=True)
    i2 = jnp.min(jnp.where(rest == m2, lanef, big), axis=-1, keepdims=True)
    e2 = jnp.exp(m2 - m1)
    w1 = 1.0 / (1.0 + e2)
    w2 = e2 / (1.0 + e2)

    onehot = jnp.where(lanef == i1, 1.0, jnp.where(lanef == i2, 1.0, 0.0))
    r = lax.broadcasted_iota(I32, (tm, tm), 0)
    c = lax.broadcasted_iota(I32, (tm, tm), 1)
    strict_lower = jnp.where(c < r, 1.0, 0.0).astype(BF16)
    before = carry_ref[0:1, :]
    prefix = _dot(strict_lower, onehot.astype(BF16)) + before
    rank1 = jnp.sum(jnp.where(lanef == i1, prefix, 0.0), axis=-1, keepdims=True)
    rank2 = jnp.sum(jnp.where(lanef == i2, prefix, 0.0), axis=-1, keepdims=True)
    after = before + jnp.sum(onehot, axis=0, keepdims=True)

    srow = lax.broadcasted_iota(I32, (SUBLANES, LANES), 0)
    cnt_ref[0] = jnp.where(srow == 0, before, jnp.where(srow == 1, after, 0.0))
    carry_ref[...] = jnp.broadcast_to(after, carry_ref.shape)

    rec = jnp.where(lane == 0, i1, jnp.where(lane == 1, i2, jnp.where(
        lane == 2, rank1, jnp.where(lane == 3, rank2, jnp.where(
            lane == 4, w1, jnp.where(lane == 5, w2, 0.0))))))
    rc_ref[...] = rec


def _router(h, g, rw_pad, rb_pad, *, n_exp):
    n, d = h.shape
    tm = _tile(MOE_CHUNK, n)
    nc = n // tm
    return pl.pallas_call(
        functools.partial(_router_kernel, n_exp=n_exp),
        grid=(nc,),
        in_specs=[pl.BlockSpec((tm, d), lambda i: (i, 0)),
                  pl.BlockSpec((1, d), lambda i: (0, 0)),
                  pl.BlockSpec((d, LANES), lambda i: (0, 0)),
                  pl.BlockSpec((1, LANES), lambda i: (0, 0))],
        out_specs=[pl.BlockSpec((tm, LANES), lambda i: (i, 0)),
                   pl.BlockSpec((1, SUBLANES, LANES), lambda i: (i, 0, 0))],
        out_shape=[jax.ShapeDtypeStruct((n, LANES), F32),
                   jax.ShapeDtypeStruct((nc, SUBLANES, LANES), F32)],
        scratch_shapes=[pltpu.VMEM((SUBLANES, LANES), F32)],
        compiler_params=_params("arbitrary"),
        name="moe_router",
    )(h, g, rw_pad, rb_pad)


def _moe_plan(rc, cnt, *, n_exp, n_tiles):
    counts = cnt[-1, 1, :n_exp].astype(I32)
    tiles_e = (counts + MOE_TILE - 1) // MOE_TILE
    tile_end = jnp.cumsum(tiles_e)
    n_used = tile_end[-1]
    pad_start = (tile_end - tiles_e) * MOE_TILE
    t_ids = jnp.arange(n_tiles, dtype=I32)
    tile_e = jnp.minimum(jnp.searchsorted(tile_end, t_ids, side="right"), n_exp - 1).astype(I32)
    tile_e = jnp.where(t_ids < n_used, tile_e, tile_e[n_used - 1])
    dest1 = pad_start[rc[:, 0].astype(I32)] + rc[:, 2].astype(I32)
    dest2 = pad_start[rc[:, 1].astype(I32)] + rc[:, 3].astype(I32)
    return dict(tile_e=tile_e, n_used=n_used.reshape(1).astype(I32), dest1=dest1.astype(I32), dest2=dest2.astype(I32))


def _row_tile(ref, row):
    return ref.at[pl.ds(pl.multiple_of(row * SUBLANES, SUBLANES), SUBLANES), :]


def _wait_rows(src_ref, dst_ref, sem, n_rows):
    n = n_rows * SUBLANES
    pltpu.make_async_copy(src_ref.at[pl.ds(0, n), :], dst_ref.at[pl.ds(0, n), :], sem).wait()


def _dispatch_kernel(d1_ref, d2_ref, a_hbm, zeros_hbm, xb_hbm, sem, *, chunk):
    del zeros_hbm
    base = pl.program_id(0) * chunk

    def body(i, carry):
        t = base + i
        src = _row_tile(a_hbm, t)
        pltpu.make_async_copy(src, _row_tile(xb_hbm, d1_ref[t]), sem).start()
        pltpu.make_async_copy(src, _row_tile(xb_hbm, d2_ref[t]), sem).start()
        return carry

    lax.fori_loop(0, chunk, body, 0)
    _wait_rows(a_hbm, xb_hbm, sem, TOP_K * chunk)


def _moe_dispatch(plan, a_packed, *, n_rows):
    n = a_packed.shape[0] // SUBLANES
    chunk = _tile(MOE_CHUNK, n)
    any_spec = pl.BlockSpec(memory_space=pl.ANY)
    spec = pltpu.PrefetchScalarGridSpec(
        num_scalar_prefetch=2,
        grid=(n // chunk,),
        in_specs=[any_spec, any_spec],
        out_specs=any_spec,
        scratch_shapes=[pltpu.SemaphoreType.DMA(())],
    )
    zeros = jnp.zeros((n_rows * SUBLANES, LANES), U32)
    return pl.pallas_call(
        functools.partial(_dispatch_kernel, chunk=chunk),
        grid_spec=spec,
        out_shape=jax.ShapeDtypeStruct(zeros.shape, U32),
        input_output_aliases={3: 0},
        compiler_params=_params("arbitrary"),
        name="moe_dispatch",
    )(plan["dest1"], plan["dest2"], a_packed, zeros)


def _expert_kernel(te_ref, nu_ref, x_ref, wg_ref, wu_ref, wd_ref, y_ref, acc_ref):
    t = pl.program_id(0)
    f = pl.program_id(1)
    rows = acc_ref.shape[0]

    @pl.when(t < nu_ref[0])
    def _():
        @pl.when(f == 0)
        def _():
            acc_ref[...] = jnp.zeros(acc_ref.shape, F32)

        lo, hi = _unpack_rows(x_ref, rows)
        x = jnp.concatenate([lo.astype(BF16), hi.astype(BF16)], axis=1)
        hid = (jax.nn.silu(_dot(x, wg_ref[0])) * _dot(x, wu_ref[0])).astype(BF16)
        acc_ref[...] += _dot(hid, wd_ref[0])

        @pl.when(f == pl.num_programs(1) - 1)
        def _():
            _pack_rows(y_ref, acc_ref[...])

    @pl.when((t >= nu_ref[0]) & (f == pl.num_programs(1) - 1))
    def _():
        y_ref[...] = jnp.zeros(y_ref.shape, y_ref.dtype)


def _moe_experts(plan, xb, wg, wu, wd):
    d, ff = wg.shape[1], wg.shape[2]
    tile = MOE_TILE
    n_tiles = xb.shape[0] // (tile * SUBLANES)
    tf = _tile(MOE_COLS, ff)
    nf = ff // tf

    def fcol(t, f, nu):
        return jnp.where(t < nu[0], f, nf - 1)

    spec = pltpu.PrefetchScalarGridSpec(
        num_scalar_prefetch=2,
        grid=(n_tiles, nf),
        in_specs=[pl.BlockSpec((tile * SUBLANES, LANES), lambda t, f, te, nu: (jnp.minimum(t, nu[0] - 1), 0)),
                  pl.BlockSpec((1, d, tf), lambda t, f, te, nu: (te[t], 0, fcol(t, f, nu))),
                  pl.BlockSpec((1, d, tf), lambda t, f, te, nu: (te[t], 0, fcol(t, f, nu))),
                  pl.BlockSpec((1, tf, d), lambda t, f, te, nu: (te[t], fcol(t, f, nu), 0))],
        out_specs=pl.BlockSpec((tile * SUBLANES, LANES), lambda t, f, te, nu: (t, 0)),
        scratch_shapes=[pltpu.VMEM((tile, d), F32)],
    )
    return pl.pallas_call(
        _expert_kernel,
        grid_spec=spec,
        out_shape=jax.ShapeDtypeStruct(xb.shape, U32),
        compiler_params=_params("arbitrary", "arbitrary"),
        name="moe_experts",
    )(plan["tile_e"], plan["n_used"], xb, wg, wu, wd)


def _combine_kernel(d1_ref, d2_ref, rc_ref, h_ref, g_ref, y_hbm, *rest, final):
    outs, (y1_ref, y2_ref, sem) = rest[:-3], rest[-3:]
    chunk = h_ref.shape[0]
    base = pl.program_id(0) * chunk

    def body(i, carry):
        t = base + i
        pltpu.make_async_copy(_row_tile(y_hbm, d1_ref[t]), _row_tile(y1_ref, i), sem).start()
        pltpu.make_async_copy(_row_tile(y_hbm, d2_ref[t]), _row_tile(y2_ref, i), sem).start()
        return carry

    lax.fori_loop(0, chunk, body, 0)
    _wait_rows(y_hbm, y1_ref, sem, TOP_K * chunk)
    rc = rc_ref[...]
    w1, w2 = rc[:, 4:5], rc[:, 5:6]
    lo1, hi1 = _unpack_rows(y1_ref, chunk)
    lo2, hi2 = _unpack_rows(y2_ref, chunk)
    y = jnp.concatenate([w1 * lo1 + w2 * lo2, w1 * hi1 + w2 * hi2], axis=1)
    _write_epilogue(h_ref[...] + y, g_ref[...], outs, final)


def _moe_combine(plan, rc, yb, h, g, *, final):
    n, d = h.shape
    chunk = _tile(MOE_CHUNK, n)
    tok = lambda c, d1, d2: (c, 0)
    shapes = _epilogue_shapes(n, d, final)
    spec = pltpu.PrefetchScalarGridSpec(
        num_scalar_prefetch=2,
        grid=(n // chunk,),
        in_specs=[pl.BlockSpec((chunk, LANES), tok),
                  pl.BlockSpec((chunk, d), tok),
                  pl.BlockSpec((1, d), lambda c, d1, d2: (0, 0)),
                  pl.BlockSpec(memory_space=pl.ANY)],
        out_specs=[pl.BlockSpec((chunk, d), tok) for _ in shapes],
        scratch_shapes=[pltpu.VMEM((chunk * SUBLANES, LANES), U32),
                        pltpu.VMEM((chunk * SUBLANES, LANES), U32),
                        pltpu.SemaphoreType.DMA(())],
    )
    return pl.pallas_call(
        functools.partial(_combine_kernel, final=final),
        grid_spec=spec,
        out_shape=shapes,
        compiler_params=_params("arbitrary"),
        name="moe_combine",
    )(plan["dest1"], plan["dest2"], rc, h, g, yb)


def _moe_ffn(a_packed, h, g_ffn, g_next, rw, rb, wg, wu, wd, *, final):
    n, d = h.shape
    n_exp = rw.shape[1]
    assert n_exp <= LANES and TOP_K == 2
    rw_pad = jnp.zeros((d, LANES), F32).at[:, :n_exp].set(rw)
    rb_pad = jnp.zeros((1, LANES), F32).at[0, :n_exp].set(rb)
    rc, cnt = _router(h, g_ffn, rw_pad, rb_pad, n_exp=n_exp)
    n_tiles = (n * TOP_K) // MOE_TILE + n_exp
    plan = _moe_plan(rc, cnt, n_exp=n_exp, n_tiles=n_tiles)
    xb = _moe_dispatch(plan, a_packed, n_rows=n_tiles * MOE_TILE)
    yb = _moe_experts(plan, xb, wg, wu, wd)
    return _moe_combine(plan, rc, yb, h, g_next, final=final)


def kernel(x, mix_norm, w_in, b_forget, conv_w, w_conv_out, w_attn_out, w_o, ffn_norm, dense_w_gate, dense_w_up, dense_w_down, router_w, router_b, moe_w_gate, moe_w_up, moe_w_down, final_norm):
    batch, seq, d = x.shape
    depth = mix_norm.shape[0]
    conv_k, d_conv = conv_w.shape[1], conv_w.shape[2]
    d_attn = w_attn_out.shape[1]
    n_heads = b_forget.shape[1]
    head_dim = d_attn // n_heads
    assert head_dim % LANES == 0 and conv_k - 1 <= SUBLANES
    n = batch * seq
    hp = 2 * SUBLANES
    assert n_heads <= hp
    scale = LOG2_E / math.sqrt(head_dim)
    o_q = 3 * d_conv
    o_f = o_q + 3 * d_attn
    o_g = o_f + n_heads

    h = x.reshape(n, d)
    a = _norm(h, mix_norm[0].reshape(1, d))
    out = None
    for layer in range(depth):
        wl = w_in[layer]
        w_bcu = wl[:, :o_q].astype(BF16)
        w_qkv = jnp.concatenate([wl[:, o_q:o_q + d_attn] * scale, wl[:, o_q + d_attn:o_f]], axis=1).astype(BF16)
        wf_t = jnp.zeros((hp, d), F32).at[:n_heads].set(wl[:, o_f:o_g].T).astype(BF16)
        bf_col = jnp.zeros((hp, 1), F32).at[:n_heads, 0].set(b_forget[layer])
        w_gates = wl[:, o_g:].astype(BF16)
        cw_pad = jnp.zeros((SUBLANES, d_conv), F32).at[:conv_k].set(conv_w[layer])

        z = _conv_front(a, w_bcu, cw_pad, d_conv=d_conv, seq=seq, conv_k=conv_k)
        qkv = _proj(a, w_qkv, sigmoid=False, name="qkv_proj")
        gates = _proj(a, w_gates, sigmoid=True, name="gate_proj")
        c = _forget_cumsum(a, wf_t, bf_col, batch=batch, seq=seq)
        o = _attention(qkv, c, batch=batch, seq=seq, n_heads=n_heads, head_dim=head_dim)
        h, a = _mix_out(z, o, gates, h, w_conv_out[layer].astype(BF16), w_attn_out[layer].astype(BF16),
                        w_o[layer].astype(BF16), ffn_norm[layer].reshape(1, d), packed=layer % 2 == 1)

        final = layer == depth - 1
        g_next = (final_norm if final else mix_norm[layer + 1]).reshape(1, d)
        i = layer // 2
        if layer % 2 == 0:
            res = _dense_ffn(a, h, dense_w_gate[i].astype(BF16), dense_w_up[i].astype(BF16),
                             dense_w_down[i].astype(BF16), g_next, final=final)
        else:
            res = _moe_ffn(a, h, ffn_norm[layer].reshape(1, d), g_next, router_w[i], router_b[i],
                           moe_w_gate[i].astype(BF16), moe_w_up[i].astype(BF16),
                           moe_w_down[i].astype(BF16), final=final)
        if final:
            out = res[0]
        else:
            h, a = res
    return out.reshape(batch, seq, d)
```

```python
import functools
import math

import jax
import jax.numpy as jnp
from jax import lax
from jax.experimental import pallas as pl
from jax.experimental.pallas import tpu as pltpu

F32 = jnp.float32
BF16 = jnp.bfloat16
I32 = jnp.int32
U32 = jnp.uint32

RMS_EPS = 1e-6
LOG2_E = math.log2(math.e)
TOP_K = 2
LANES = 128
SUBLANES = 8
VMEM_LIMIT = 60 * 1024 * 1024

PROJ_ROWS = 1024
PROJ_COLS = 1024
CONV_COLS = 512
CUM_ROWS = 512
ATTN_TILE = 512
ATTN_UNROLL = 8
MIX_ROWS = 256
FFN_ROWS = 512
FFN_COLS = 512
MOE_CHUNK = 512
MOE_TILE = 1024
MOE_COLS = 1024


def _tile(pref, dim):
    t = min(pref, dim)
    assert dim % t == 0, (pref, dim)
    return t


def _params(*sem):
    return pltpu.CompilerParams(dimension_semantics=sem, vmem_limit_bytes=VMEM_LIMIT)


def _dot(a, b):
    return jnp.dot(a, b, preferred_element_type=F32)


def _dot_nt(a, b):
    return lax.dot_general(a, b, (((1,), (1,)), ((), ())), preferred_element_type=F32)


def _split3(x):
    hi = x.astype(BF16)
    r = x - hi.astype(F32)
    mid = r.astype(BF16)
    lo = (r - mid.astype(F32)).astype(BF16)
    return hi, mid, lo


def _dot_f32_by_01(x, onehot):
    hi, mid, lo = _split3(x)
    return _dot(hi, onehot) + _dot(mid, onehot) + _dot(lo, onehot)


def _dot_f32(a, b):
    a1, a2, a3 = _split3(a)
    b1, b2, b3 = _split3(b)
    small = _dot(a1, b3) + _dot(a3, b1) + _dot(a2, b2)
    mid = _dot(a1, b2) + _dot(a2, b1)
    return _dot(a1, b1) + (mid + small)


def _rms(xf, g):
    ms = jnp.mean(xf * xf, axis=-1, keepdims=True)
    return xf * lax.rsqrt(ms + RMS_EPS) * g


def _pack_rows(ref, x):
    m, d = x.shape
    half = d // 2
    assert half == SUBLANES * LANES
    lo = lax.bitcast_convert_type(x[:, :half].astype(BF16).astype(F32), U32) >> 16
    hi = lax.bitcast_convert_type(x[:, half:].astype(BF16).astype(F32), U32) & jnp.uint32(0xFFFF0000)
    w = hi | lo
    for r in range(SUBLANES):
        ref[pl.ds(r, m, stride=SUBLANES), :] = w[:, r * LANES:(r + 1) * LANES]


def _unpack_rows(ref, m):
    w = jnp.concatenate([ref[pl.ds(r, m, stride=SUBLANES), :] for r in range(SUBLANES)], axis=1)
    lo = lax.bitcast_convert_type(w << 16, F32)
    hi = lax.bitcast_convert_type(w & jnp.uint32(0xFFFF0000), F32)
    return lo, hi


def _norm_kernel(x_ref, g_ref, a_ref):
    a_ref[...] = _rms(x_ref[...], g_ref[...]).astype(a_ref.dtype)


def _norm(x, g):
    n, d = x.shape
    tm = _tile(FFN_ROWS, n)
    return pl.pallas_call(
        _norm_kernel,
        grid=(n // tm,),
        in_specs=[pl.BlockSpec((tm, d), lambda i: (i, 0)),
                  pl.BlockSpec((1, d), lambda i: (0, 0))],
        out_specs=pl.BlockSpec((tm, d), lambda i: (i, 0)),
        out_shape=jax.ShapeDtypeStruct((n, d), BF16),
        compiler_params=_params("parallel"),
        name="rmsnorm",
    )(x, g)


def _proj_kernel(a_ref, w_ref, o_ref, *, sigmoid):
    y = _dot(a_ref[...], w_ref[...])
    if sigmoid:
        y = jax.nn.sigmoid(y)
    o_ref[...] = y.astype(o_ref.dtype)


def _proj(a, w, *, sigmoid, name):
    n, k = a.shape
    p = w.shape[1]
    tm = _tile(PROJ_ROWS, n)
    tn = _tile(PROJ_COLS, p)
    return pl.pallas_call(
        functools.partial(_proj_kernel, sigmoid=sigmoid),
        grid=(n // tm, p // tn),
        in_specs=[pl.BlockSpec((tm, k), lambda i, j: (i, 0)),
                  pl.BlockSpec((k, tn), lambda i, j: (0, j))],
        out_specs=pl.BlockSpec((tm, tn), lambda i, j: (i, j)),
        out_shape=jax.ShapeDtypeStruct((n, p), BF16),
        compiler_params=_params("parallel", "arbitrary"),
        name=name,
    )(a, w)


def _conv_kernel(a_ref, wb_ref, wc_ref, wu_ref, cw_ref, z_ref, carry_ref, *, tiles_per_seq, conv_k):
    i = pl.program_id(0)
    j = pl.program_id(1)

    @pl.when(i % tiles_per_seq == 0)
    def _():
        carry_ref[j] = jnp.zeros(carry_ref.shape[1:], F32)

    a = a_ref[...]
    b = _dot(a, wb_ref[...])
    cu = _dot(a, wc_ref[...]) * _dot(a, wu_ref[...])
    tm = cu.shape[0]
    prev = carry_ref[j]
    row = lax.broadcasted_iota(I32, cu.shape, 0)
    cw = cw_ref[...]
    y = cu * cw[conv_k - 1:conv_k]
    for s in range(1, conv_k):
        sh = pltpu.roll(cu, s, 0)
        for r in range(s):
            src = SUBLANES - s + r
            sh = jnp.where(row == r, prev[src:src + 1], sh)
        y = y + sh * cw[conv_k - 1 - s:conv_k - s]
    z_ref[...] = (b * y).astype(z_ref.dtype)
    carry_ref[j] = cu[tm - SUBLANES:tm]


def _conv_front(a, w_bcu, conv_w_pad, *, d_conv, seq, conv_k):
    n, k = a.shape
    tm = _tile(PROJ_ROWS, seq)
    tn = _tile(CONV_COLS, d_conv)
    nj = d_conv // tn
    return pl.pallas_call(
        functools.partial(_conv_kernel, tiles_per_seq=seq // tm, conv_k=conv_k),
        grid=(n // tm, nj),
        in_specs=[pl.BlockSpec((tm, k), lambda i, j: (i, 0)),
                  pl.BlockSpec((k, tn), lambda i, j: (0, j)),
                  pl.BlockSpec((k, tn), lambda i, j: (0, nj + j)),
                  pl.BlockSpec((k, tn), lambda i, j: (0, 2 * nj + j)),
                  pl.BlockSpec((SUBLANES, tn), lambda i, j: (0, j))],
        out_specs=pl.BlockSpec((tm, tn), lambda i, j: (i, j)),
        out_shape=jax.ShapeDtypeStruct((n, d_conv), BF16),
        scratch_shapes=[pltpu.VMEM((nj, SUBLANES, tn), F32)],
        compiler_params=_params("arbitrary", "arbitrary"),
        name="conv_front",
    )(a, w_bcu, w_bcu, w_bcu, conv_w_pad)


def _cum_kernel(a_ref, wf_ref, bf_ref, c_ref, carry_ref):
    si = pl.program_id(1)

    @pl.when(si == 0)
    def _():
        carry_ref[...] = jnp.zeros(carry_ref.shape, F32)

    lf = jax.nn.log_sigmoid(_dot_nt(wf_ref[...], a_ref[...]) + bf_ref[...])
    tm = lf.shape[1]
    r = lax.broadcasted_iota(I32, (tm, tm), 0)
    c = lax.broadcasted_iota(I32, (tm, tm), 1)
    upper = jnp.where(r <= c, 1.0, 0.0).astype(BF16)
    cs = _dot_f32_by_01(lf, upper) + carry_ref[:, 0:1]
    c_ref[0] = cs * LOG2_E
    carry_ref[...] = jnp.broadcast_to(cs[:, tm - 1:tm], carry_ref.shape)


def _forget_cumsum(a, wf_t, bf_col, *, batch, seq):
    n, k = a.shape
    hp = wf_t.shape[0]
    tm = _tile(CUM_ROWS, seq)
    ns = seq // tm
    return pl.pallas_call(
        _cum_kernel,
        grid=(batch, ns),
        in_specs=[pl.BlockSpec((tm, k), lambda b, s: (b * ns + s, 0)),
                  pl.BlockSpec((hp, k), lambda b, s: (0, 0)),
                  pl.BlockSpec((hp, 1), lambda b, s: (0, 0))],
        out_specs=pl.BlockSpec((1, hp, tm), lambda b, s: (b, 0, s)),
        out_shape=jax.ShapeDtypeStruct((batch, hp, seq), F32),
        scratch_shapes=[pltpu.VMEM((hp, LANES), F32)],
        compiler_params=_params("arbitrary", "arbitrary"),
        name="forget_cumsum",
    )(a, wf_t, bf_col)


def _attn_kernel(q_ref, k_ref, v_ref, c_ref, o_ref, s_ref, p_ref, al_ref, m_ref, acc_ref, *, t):
    qi = pl.program_id(2)
    dh = q_ref.shape[1]
    reps = t // LANES
    q0 = pl.multiple_of(qi * t, t)
    c0 = c_ref[0, 0, :, pl.ds(q0, LANES)][:, 0:1]
    ones_col = jnp.where(lax.broadcasted_iota(I32, (t, LANES), 1) == 0, 1.0, 0.0).astype(BF16)

    def score(slot, k0, masked):
        s = _dot_nt(q_ref[...], k_ref[pl.ds(k0, t), :]) - (c_ref[0, 0, :, pl.ds(k0, t)] - c0)
        if masked:
            row = lax.broadcasted_iota(I32, (t, t), 0)
            col = lax.broadcasted_iota(I32, (t, t), 1)
            s = jnp.where(col <= row, s, -jnp.inf)
        s_ref[slot] = s

    def softmax(slot):
        s = s_ref[slot]
        m_prev = m_ref[...]
        m_new = jnp.maximum(m_prev, jnp.max(s, axis=-1, keepdims=True))
        al_ref[slot] = jnp.exp2(m_prev - m_new)
        p_ref[slot] = jnp.exp2(s - jnp.concatenate([m_new] * reps, axis=1)).astype(BF16)
        m_ref[...] = m_new

    def apply(slot, k0):
        vaug = jnp.concatenate([v_ref[pl.ds(k0, t), :], ones_col], axis=1)
        al = jnp.concatenate([al_ref[slot]] * (acc_ref.shape[1] // LANES), axis=1)
        acc_ref[...] = al * acc_ref[...] + _dot(p_ref[slot], vaug)

    def key_start(item):
        return pl.multiple_of(jnp.where(item == 0, q0, jnp.maximum(item - 1, 0) * t), t)

    m_ref[...] = jnp.full(m_ref.shape, -jnp.inf, F32)
    acc_ref[...] = jnp.zeros(acc_ref.shape, F32)
    p_ref[1] = jnp.zeros(p_ref.shape[1:], BF16)
    al_ref[1] = jnp.ones(al_ref.shape[1:], F32)
    score(0, q0, True)

    def step(i, slot):
        score(1 - slot, pl.multiple_of(i * t, t), False)
        softmax(slot)
        apply(1 - slot, key_start(i - 1))

    def drain(last, slot):
        softmax(slot)
        apply(1 - slot, key_start(last - 1))
        apply(slot, key_start(last))

    def body(g, carry):
        for u in range(ATTN_UNROLL):
            step(g * ATTN_UNROLL + u, u % 2)
        return carry

    last = qi
    groups = last // ATTN_UNROLL
    lax.fori_loop(0, groups, body, 0)
    base = groups * ATTN_UNROLL
    for rem in range(ATTN_UNROLL):
        @pl.when(last - base == rem)
        def _(rem=rem):
            for u in range(rem):
                step(base + u, u % 2)
            drain(last, rem % 2)

    acc = acc_ref[...]
    o_ref[...] = (acc[:, :dh] / acc[:, dh:dh + 1]).astype(o_ref.dtype)


def _attention(qkv, c, *, batch, seq, n_heads, head_dim):
    n = qkv.shape[0]
    t = _tile(ATTN_TILE, seq)
    nq = seq // t
    c = c.reshape(batch, c.shape[1], 1, seq)
    return pl.pallas_call(
        functools.partial(_attn_kernel, t=t),
        grid=(batch, n_heads, nq),
        in_specs=[pl.BlockSpec((t, head_dim), lambda b, h, i: (b * nq + i, h)),
                  pl.BlockSpec((seq, head_dim), lambda b, h, i: (b, n_heads + h)),
                  pl.BlockSpec((seq, head_dim), lambda b, h, i: (b, 2 * n_heads + h)),
                  pl.BlockSpec((1, 1, 1, seq), lambda b, h, i: (b, h, 0, 0))],
        out_specs=pl.BlockSpec((t, head_dim), lambda b, h, i: (b * nq + i, h)),
        out_shape=jax.ShapeDtypeStruct((n, n_heads * head_dim), BF16),
        scratch_shapes=[pltpu.VMEM((2, t, t), F32), pltpu.VMEM((2, t, t), BF16),
                        pltpu.VMEM((2, t, LANES), F32), pltpu.VMEM((t, LANES), F32),
                        pltpu.VMEM((t, head_dim + LANES), F32)],
        compiler_params=_params("parallel", "parallel", "arbitrary"),
        name="forget_attn",
    )(qkv, qkv, qkv, c)


def _mix_kernel(z_ref, o_ref, ga_ref, gb_ref, h_ref, wc_ref, wa_ref, wo_ref, g_ref, hn_ref, a_ref, *, packed):
    yc = _dot(z_ref[...], wc_ref[...])
    ya = _dot(o_ref[...], wa_ref[...])
    m = ga_ref[...].astype(F32) * yc + gb_ref[...].astype(F32) * ya
    hn = h_ref[...] + _dot(m.astype(BF16), wo_ref[...])
    hn_ref[...] = hn
    a = _rms(hn, g_ref[...])
    if packed:
        _pack_rows(a_ref, a)
    else:
        a_ref[...] = a.astype(a_ref.dtype)


def _mix_out(z, o, gates, h, wc, wa, wo, g, *, packed):
    n, d = h.shape
    dc = z.shape[1]
    da = o.shape[1]
    tm = _tile(MIX_ROWS, n)
    const = lambda i: (0, 0)
    if packed:
        a_spec = pl.BlockSpec((tm * SUBLANES, LANES), lambda i: (i, 0))
        a_shape = jax.ShapeDtypeStruct((n * SUBLANES, LANES), U32)
    else:
        a_spec = pl.BlockSpec((tm, d), lambda i: (i, 0))
        a_shape = jax.ShapeDtypeStruct((n, d), BF16)
    return pl.pallas_call(
        functools.partial(_mix_kernel, packed=packed),
        grid=(n // tm,),
        in_specs=[pl.BlockSpec((tm, dc), lambda i: (i, 0)),
                  pl.BlockSpec((tm, da), lambda i: (i, 0)),
                  pl.BlockSpec((tm, d), lambda i: (i, 0)),
                  pl.BlockSpec((tm, d), lambda i: (i, 1)),
                  pl.BlockSpec((tm, d), lambda i: (i, 0)),
                  pl.BlockSpec((dc, d), const),
                  pl.BlockSpec((da, d), const),
                  pl.BlockSpec((d, d), const),
                  pl.BlockSpec((1, d), const)],
        out_specs=[pl.BlockSpec((tm, d), lambda i: (i, 0)), a_spec],
        out_shape=[jax.ShapeDtypeStruct((n, d), F32), a_shape],
        input_output_aliases={4: 0},
        compiler_params=_params("parallel"),
        name="mix_out",
    )(z, o, gates, gates, h, wc, wa, wo, g)


def _epilogue_shapes(n, d, final):
    if final:
        return [jax.ShapeDtypeStruct((n, d), F32)]
    return [jax.ShapeDtypeStruct((n, d), F32), jax.ShapeDtypeStruct((n, d), BF16)]


def _write_epilogue(hn, g, outs, final):
    if final:
        outs[0][...] = _rms(hn, g)
    else:
        outs[0][...] = hn
        outs[1][...] = _rms(hn, g).astype(outs[1].dtype)


def _ffn_kernel(a_ref, wg_ref, wu_ref, wd_ref, h_ref, g_ref, *rest, final):
    outs, acc_ref = rest[:-1], rest[-1]
    f = pl.program_id(1)

    @pl.when(f == 0)
    def _():
        acc_ref[...] = h_ref[...]

    a = a_ref[...]
    hid = (jax.nn.silu(_dot(a, wg_ref[...])) * _dot(a, wu_ref[...])).astype(BF16)
    acc_ref[...] += _dot(hid, wd_ref[...])

    @pl.when(f == pl.num_programs(1) - 1)
    def _():
        _write_epilogue(acc_ref[...], g_ref[...], outs, final)


def _dense_ffn(a, h, wg, wu, wd, g, *, final):
    n, d = h.shape
    ff = wg.shape[1]
    tm = _tile(FFN_ROWS, n)
    tf = _tile(FFN_COLS, ff)
    row = lambda i, f: (i, 0)
    shapes = _epilogue_shapes(n, d, final)
    return pl.pallas_call(
        functools.partial(_ffn_kernel, final=final),
        grid=(n // tm, ff // tf),
        in_specs=[pl.BlockSpec((tm, d), row),
                  pl.BlockSpec((d, tf), lambda i, f: (0, f)),
                  pl.BlockSpec((d, tf), lambda i, f: (0, f)),
                  pl.BlockSpec((tf, d), lambda i, f: (f, 0)),
                  pl.BlockSpec((tm, d), row),
                  pl.BlockSpec((1, d), lambda i, f: (0, 0))],
        out_specs=[pl.BlockSpec((tm, d), row) for _ in shapes],
        out_shape=shapes,
        scratch_shapes=[pltpu.VMEM((tm, d), F32)],
        compiler_params=_params("parallel", "arbitrary"),
        name="dense_ffn",
    )(a, wg, wu, wd, h, g)


def _router_kernel(h_ref, g_ref, rw_ref, rb_ref, rc_ref, cnt_ref, carry_ref, *, n_exp):
    i = pl.program_id(0)

    @pl.when(i == 0)
    def _():
        carry_ref[...] = jnp.zeros(carry_ref.shape, F32)

    a = _rms(h_ref[...], g_ref[...])
    tm = a.shape[0]
    lane = lax.broadcasted_iota(I32, (tm, LANES), 1)
    lanef = lane.astype(F32)
    logits = jnp.where(lane < n_exp, _dot_f32(a, rw_ref[...]) + rb_ref[...], -jnp.inf)
    big = float(LANES)
    m1 = jnp.max(logits, axis=-1, keepdims=True)
    i1 = jnp.min(jnp.where(logits == m1, lanef, big), axis=-1, keepdims=True)
    rest = jnp.where(lanef == i1, -jnp.inf, logits)
    m2 = jnp.max(rest, axis=-1, keepdims=True)
    i2 = jnp.min(jnp.where(rest == m2, lanef, big), axis=-1, keepdims=True)
    e2 = jnp.exp(m2 - m1)
    w1 = 1.0 / (1.0 + e2)
    w2 = e2 / (1.0 + e2)

    onehot = jnp.where(lanef == i1, 1.0, jnp.where(lanef == i2, 1.0, 0.0))
    r = lax.broadcasted_iota(I32, (tm, tm), 0)
    c = lax.broadcasted_iota(I32, (tm, tm), 1)
    strict_lower = jnp.where(c < r, 1.0, 0.0).astype(BF16)
    before = carry_ref[0:1, :]
    prefix = _dot(strict_lower, onehot.astype(BF16)) + before
    rank1 = jnp.sum(jnp.where(lanef == i1, prefix, 0.0), axis=-1, keepdims=True)
    rank2 = jnp.sum(jnp.where(lanef == i2, prefix, 0.0), axis=-1, keepdims=True)
    after = before + jnp.sum(onehot, axis=0, keepdims=True)

    srow = lax.broadcasted_iota(I32, (SUBLANES, LANES), 0)
    cnt_ref[0] = jnp.where(srow == 0, before, jnp.where(srow == 1, after, 0.0))
    carry_ref[...] = jnp.broadcast_to(after, carry_ref.shape)

    rec = jnp.where(lane == 0, i1, jnp.where(lane == 1, i2, jnp.where(
        lane == 2, rank1, jnp.where(lane == 3, rank2, jnp.where(
            lane == 4, w1, jnp.where(lane == 5, w2, 0.0))))))
    rc_ref[...] = rec


def _router(h, g, rw_pad, rb_pad, *, n_exp):
    n, d = h.shape
    tm = _tile(MOE_CHUNK, n)
    nc = n // tm
    return pl.pallas_call(
        functools.partial(_router_kernel, n_exp=n_exp),
        grid=(nc,),
        in_specs=[pl.BlockSpec((tm, d), lambda i: (i, 0)),
                  pl.BlockSpec((1, d), lambda i: (0, 0)),
                  pl.BlockSpec((d, LANES), lambda i: (0, 0)),
                  pl.BlockSpec((1, LANES), lambda i: (0, 0))],
        out_specs=[pl.BlockSpec((tm, LANES), lambda i: (i, 0)),
                   pl.BlockSpec((1, SUBLANES, LANES), lambda i: (i, 0, 0))],
        out_shape=[jax.ShapeDtypeStruct((n, LANES), F32),
                   jax.ShapeDtypeStruct((nc, SUBLANES, LANES), F32)],
        scratch_shapes=[pltpu.VMEM((SUBLANES, LANES), F32)],
        compiler_params=_params("arbitrary"),
        name="moe_router",
    )(h, g, rw_pad, rb_pad)


def _moe_plan(rc, cnt, *, n_exp, n_tiles):
    counts = cnt[-1, 1, :n_exp].astype(I32)
    tiles_e = (counts + MOE_TILE - 1) // MOE_TILE
    tile_end = jnp.cumsum(tiles_e)
    n_used = tile_end[-1]
    pad_start = (tile_end - tiles_e) * MOE_TILE
    t_ids = jnp.arange(n_tiles, dtype=I32)
    tile_e = jnp.minimum(jnp.searchsorted(tile_end, t_ids, side="right"), n_exp - 1).astype(I32)
    tile_e = jnp.where(t_ids < n_used, tile_e, tile_e[n_used - 1])
    dest1 = pad_start[rc[:, 0].astype(I32)] + rc[:, 2].astype(I32)
    dest2 = pad_start[rc[:, 1].astype(I32)] + rc[:, 3].astype(I32)
    return dict(tile_e=tile_e, n_used=n_used.reshape(1).astype(I32), dest1=dest1.astype(I32), dest2=dest2.astype(I32))


def _row_tile(ref, row):
    return ref.at[pl.ds(pl.multiple_of(row * SUBLANES, SUBLANES), SUBLANES), :]


def _wait_rows(src_ref, dst_ref, sem, n_rows):
    n = n_rows * SUBLANES
    pltpu.make_async_copy(src_ref.at[pl.ds(0, n), :], dst_ref.at[pl.ds(0, n), :], sem).wait()


def _dispatch_kernel(d1_ref, d2_ref, a_ref, zeros_hbm, xb_hbm, sem):
    del zeros_hbm
    chunk = a_ref.shape[0] // SUBLANES
    base = pl.program_id(0) * chunk

    def body(i, carry):
        t = base + i
        src = _row_tile(a_ref, i)
        pltpu.make_async_copy(src, _row_tile(xb_hbm, d1_ref[t]), sem).start()
        pltpu.make_async_copy(src, _row_tile(xb_hbm, d2_ref[t]), sem).start()
        return carry

    lax.fori_loop(0, chunk, body, 0)
    for _ in range(TOP_K):
        _wait_rows(a_ref, xb_hbm, sem, chunk)


def _moe_dispatch(plan, a_packed, *, n_rows):
    n = a_packed.shape[0] // SUBLANES
    chunk = _tile(MOE_CHUNK, n)
    any_spec = pl.BlockSpec(memory_space=pl.ANY)
    spec = pltpu.PrefetchScalarGridSpec(
        num_scalar_prefetch=2,
        grid=(n // chunk,),
        in_specs=[pl.BlockSpec((chunk * SUBLANES, LANES), lambda c, d1, d2: (c, 0)), any_spec],
        out_specs=any_spec,
        scratch_shapes=[pltpu.SemaphoreType.DMA(())],
    )
    zeros = jnp.zeros((n_rows * SUBLANES, LANES), U32)
    return pl.pallas_call(
        _dispatch_kernel,
        grid_spec=spec,
        out_shape=jax.ShapeDtypeStruct(zeros.shape, U32),
        input_output_aliases={3: 0},
        compiler_params=_params("arbitrary"),
        name="moe_dispatch",
    )(plan["dest1"], plan["dest2"], a_packed, zeros)


def _expert_kernel(te_ref, nu_ref, x_ref, wg_ref, wu_ref, wd_ref, y_ref, acc_ref):
    t = pl.program_id(0)
    f = pl.program_id(1)
    rows = acc_ref.shape[0]

    @pl.when(t < nu_ref[0])
    def _():
        @pl.when(f == 0)
        def _():
            acc_ref[...] = jnp.zeros(acc_ref.shape, F32)

        lo, hi = _unpack_rows(x_ref, rows)
        x = jnp.concatenate([lo.astype(BF16), hi.astype(BF16)], axis=1)
        hid = (jax.nn.silu(_dot(x, wg_ref[0])) * _dot(x, wu_ref[0])).astype(BF16)
        acc_ref[...] += _dot(hid, wd_ref[0])

        @pl.when(f == pl.num_programs(1) - 1)
        def _():
            _pack_rows(y_ref, acc_ref[...])

    @pl.when((t >= nu_ref[0]) & (f == pl.num_programs(1) - 1))
    def _():
        y_ref[...] = jnp.zeros(y_ref.shape, y_ref.dtype)


def _moe_experts(plan, xb, wg, wu, wd):
    d, ff = wg.shape[1], wg.shape[2]
    tile = MOE_TILE
    n_tiles = xb.shape[0] // (tile * SUBLANES)
    tf = _tile(MOE_COLS, ff)
    nf = ff // tf

    def fcol(t, f, nu):
        return jnp.where(t < nu[0], f, nf - 1)

    spec = pltpu.PrefetchScalarGridSpec(
        num_scalar_prefetch=2,
        grid=(n_tiles, nf),
        in_specs=[pl.BlockSpec((tile * SUBLANES, LANES), lambda t, f, te, nu: (jnp.minimum(t, nu[0] - 1), 0)),
                  pl.BlockSpec((1, d, tf), lambda t, f, te, nu: (te[t], 0, fcol(t, f, nu))),
                  pl.BlockSpec((1, d, tf), lambda t, f, te, nu: (te[t], 0, fcol(t, f, nu))),
                  pl.BlockSpec((1, tf, d), lambda t, f, te, nu: (te[t], fcol(t, f, nu), 0))],
        out_specs=pl.BlockSpec((tile * SUBLANES, LANES), lambda t, f, te, nu: (t, 0)),
        scratch_shapes=[pltpu.VMEM((tile, d), F32)],
    )
    return pl.pallas_call(
        _expert_kernel,
        grid_spec=spec,
        out_shape=jax.ShapeDtypeStruct(xb.shape, U32),
        compiler_params=_params("arbitrary", "arbitrary"),
        name="moe_experts",
    )(plan["tile_e"], plan["n_used"], xb, wg, wu, wd)


def _combine_kernel(d1_ref, d2_ref, rc_ref, h_ref, g_ref, y_hbm, *rest, final):
    outs, (y1_ref, y2_ref, sem) = rest[:-3], rest[-3:]
    chunk = h_ref.shape[0]
    base = pl.program_id(0) * chunk

    def body(i, carry):
        t = base + i
        pltpu.make_async_copy(_row_tile(y_hbm, d1_ref[t]), _row_tile(y1_ref, i), sem).start()
        pltpu.make_async_copy(_row_tile(y_hbm, d2_ref[t]), _row_tile(y2_ref, i), sem).start()
        return carry

    lax.fori_loop(0, chunk, body, 0)
    _wait_rows(y_hbm, y1_ref, sem, chunk)
    _wait_rows(y_hbm, y2_ref, sem, chunk)
    rc = rc_ref[...]
    w1, w2 = rc[:, 4:5], rc[:, 5:6]
    lo1, hi1 = _unpack_rows(y1_ref, chunk)
    lo2, hi2 = _unpack_rows(y2_ref, chunk)
    y = jnp.concatenate([w1 * lo1 + w2 * lo2, w1 * hi1 + w2 * hi2], axis=1)
    _write_epilogue(h_ref[...] + y, g_ref[...], outs, final)


def _moe_combine(plan, rc, yb, h, g, *, final):
    n, d = h.shape
    chunk = _tile(MOE_CHUNK, n)
    tok = lambda c, d1, d2: (c, 0)
    shapes = _epilogue_shapes(n, d, final)
    spec = pltpu.PrefetchScalarGridSpec(
        num_scalar_prefetch=2,
        grid=(n // chunk,),
        in_specs=[pl.BlockSpec((chunk, LANES), tok),
                  pl.BlockSpec((chunk, d), tok),
                  pl.BlockSpec((1, d), lambda c, d1, d2: (0, 0)),
                  pl.BlockSpec(memory_space=pl.ANY)],
        out_specs=[pl.BlockSpec((chunk, d), tok) for _ in shapes],
        scratch_shapes=[pltpu.VMEM((chunk * SUBLANES, LANES), U32),
                        pltpu.VMEM((chunk * SUBLANES, LANES), U32),
                        pltpu.SemaphoreType.DMA(())],
    )
    return pl.pallas_call(
        functools.partial(_combine_kernel, final=final),
        grid_spec=spec,
        out_shape=shapes,
        compiler_params=_params("arbitrary"),
        name="moe_combine",
    )(plan["dest1"], plan["dest2"], rc, h, g, yb)


def _moe_ffn(a_packed, h, g_ffn, g_next, rw, rb, wg, wu, wd, *, final):
    n, d = h.shape
    n_exp = rw.shape[1]
    assert n_exp <= LANES and TOP_K == 2
    rw_pad = jnp.zeros((d, LANES), F32).at[:, :n_exp].set(rw)
    rb_pad = jnp.zeros((1, LANES), F32).at[0, :n_exp].set(rb)
    rc, cnt = _router(h, g_ffn, rw_pad, rb_pad, n_exp=n_exp)
    n_tiles = (n * TOP_K) // MOE_TILE + n_exp
    plan = _moe_plan(rc, cnt, n_exp=n_exp, n_tiles=n_tiles)
    xb = _moe_dispatch(plan, a_packed, n_rows=n_tiles * MOE_TILE)
    yb = _moe_experts(plan, xb, wg, wu, wd)
    return _moe_combine(plan, rc, yb, h, g_next, final=final)


def kernel(x, mix_norm, w_in, b_forget, conv_w, w_conv_out, w_attn_out, w_o, ffn_norm, dense_w_gate, dense_w_up, dense_w_down, router_w, router_b, moe_w_gate, moe_w_up, moe_w_down, final_norm):
    batch, seq, d = x.shape
    depth = mix_norm.shape[0]
    conv_k, d_conv = conv_w.shape[1], conv_w.shape[2]
    d_attn = w_attn_out.shape[1]
    n_heads = b_forget.shape[1]
    head_dim = d_attn // n_heads
    assert head_dim % LANES == 0 and conv_k - 1 <= SUBLANES
    n = batch * seq
    hp = 2 * SUBLANES
    assert n_heads <= hp
    scale = LOG2_E / math.sqrt(head_dim)
    o_q = 3 * d_conv
    o_f = o_q + 3 * d_attn
    o_g = o_f + n_heads

    h = x.reshape(n, d)
    a = _norm(h, mix_norm[0].reshape(1, d))
    out = None
    for layer in range(depth):
        wl = w_in[layer]
        w_bcu = wl[:, :o_q].astype(BF16)
        w_qkv = jnp.concatenate([wl[:, o_q:o_q + d_attn] * scale, wl[:, o_q + d_attn:o_f]], axis=1).astype(BF16)
        wf_t = jnp.zeros((hp, d), F32).at[:n_heads].set(wl[:, o_f:o_g].T).astype(BF16)
        bf_col = jnp.zeros((hp, 1), F32).at[:n_heads, 0].set(b_forget[layer])
        w_gates = wl[:, o_g:].astype(BF16)
        cw_pad = jnp.zeros((SUBLANES, d_conv), F32).at[:conv_k].set(conv_w[layer])

        z = _conv_front(a, w_bcu, cw_pad, d_conv=d_conv, seq=seq, conv_k=conv_k)
        qkv = _proj(a, w_qkv, sigmoid=False, name="qkv_proj")
        gates = _proj(a, w_gates, sigmoid=True, name="gate_proj")
        c = _forget_cumsum(a, wf_t, bf_col, batch=batch, seq=seq)
        o = _attention(qkv, c, batch=batch, seq=seq, n_heads=n_heads, head_dim=head_dim)
        h, a = _mix_out(z, o, gates, h, w_conv_out[layer].astype(BF16), w_attn_out[layer].astype(BF16),
                        w_o[layer].astype(BF16), ffn_norm[layer].reshape(1, d), packed=layer % 2 == 1)

        final = layer == depth - 1
        g_next = (final_norm if final else mix_norm[layer + 1]).reshape(1, d)
        i = layer // 2
        if layer % 2 == 0:
            res = _dense_ffn(a, h, dense_w_gate[i].astype(BF16), dense_w_up[i].astype(BF16),
                             dense_w_down[i].astype(BF16), g_next, final=final)
        else:
            res = _moe_ffn(a, h, ffn_norm[layer].reshape(1, d), g_next, router_w[i], router_b[i],
                           moe_w_gate[i].astype(BF16), moe_w_up[i].astype(BF16),
                           moe_w_down[i].astype(BF16), final=final)
        if final:
            out = res[0]
        else:
            h, a = res
    return out.reshape(batch, seq, d)
```

```python
import functools
import math

import jax
import jax.numpy as jnp
from jax import lax
from jax.experimental import pallas as pl
from jax.experimental.pallas import tpu as pltpu

F32 = jnp.float32
BF16 = jnp.bfloat16
I32 = jnp.int32
U32 = jnp.uint32

RMS_EPS = 1e-6
LOG2_E = math.log2(math.e)
TOP_K = 2
LANES = 128
SUBLANES = 8
VMEM_LIMIT = 60 * 1024 * 1024

PROJ_ROWS = 1024
PROJ_COLS = 1024
CONV_COLS = 512
ATTN_TILE = 512
ATTN_UNROLL = 8
MIX_ROWS = 512
FFN_ROWS = 512
FFN_COLS = 512
MOE_CHUNK = 512
MOE_TILE = 1024
MOE_COLS = 1024


def _tile(pref, dim):
    t = min(pref, dim)
    assert dim % t == 0, (pref, dim)
    return t


def _params(*sem):
    return pltpu.CompilerParams(dimension_semantics=sem, vmem_limit_bytes=VMEM_LIMIT)


def _dot(a, b):
    return jnp.dot(a, b, preferred_element_type=F32)


def _dot_nt(a, b):
    return lax.dot_general(a, b, (((1,), (1,)), ((), ())), preferred_element_type=F32)


def _split3(x):
    hi = x.astype(BF16)
    r = x - hi.astype(F32)
    mid = r.astype(BF16)
    lo = (r - mid.astype(F32)).astype(BF16)
    return hi, mid, lo


def _dot_f32_by_01(x, onehot):
    hi, mid, lo = _split3(x)
    return _dot(hi, onehot) + _dot(mid, onehot) + _dot(lo, onehot)


def _dot_f32(a, b):
    a1, a2, a3 = _split3(a)
    b1, b2, b3 = _split3(b)
    small = _dot(a1, b3) + _dot(a3, b1) + _dot(a2, b2)
    mid = _dot(a1, b2) + _dot(a2, b1)
    return _dot(a1, b1) + (mid + small)


def _rms(xf, g):
    ms = jnp.mean(xf * xf, axis=-1, keepdims=True)
    return xf * lax.rsqrt(ms + RMS_EPS) * g


def _pack_rows(ref, x):
    m, d = x.shape
    half = d // 2
    assert half == SUBLANES * LANES
    lo = lax.bitcast_convert_type(x[:, :half].astype(BF16).astype(F32), U32) >> 16
    hi = lax.bitcast_convert_type(x[:, half:].astype(BF16).astype(F32), U32) & jnp.uint32(0xFFFF0000)
    w = hi | lo
    for r in range(SUBLANES):
        ref[pl.ds(r, m, stride=SUBLANES), :] = w[:, r * LANES:(r + 1) * LANES]


def _unpack_rows(ref, m):
    w = jnp.concatenate([ref[pl.ds(r, m, stride=SUBLANES), :] for r in range(SUBLANES)], axis=1)
    lo = lax.bitcast_convert_type(w << 16, F32)
    hi = lax.bitcast_convert_type(w & jnp.uint32(0xFFFF0000), F32)
    return lo, hi


def _norm_kernel(x_ref, g_ref, a_ref):
    a_ref[...] = _rms(x_ref[...], g_ref[...]).astype(a_ref.dtype)


def _norm(x, g):
    n, d = x.shape
    tm = _tile(FFN_ROWS, n)
    return pl.pallas_call(
        _norm_kernel,
        grid=(n // tm,),
        in_specs=[pl.BlockSpec((tm, d), lambda i: (i, 0)),
                  pl.BlockSpec((1, d), lambda i: (0, 0))],
        out_specs=pl.BlockSpec((tm, d), lambda i: (i, 0)),
        out_shape=jax.ShapeDtypeStruct((n, d), BF16),
        compiler_params=_params("parallel"),
        name="rmsnorm",
    )(x, g)


def _gate_kernel(a_ref, w_ref, o_ref):
    o_ref[...] = jax.nn.sigmoid(_dot(a_ref[...], w_ref[...])).astype(o_ref.dtype)


def _gate_proj(a, w):
    n, k = a.shape
    p = w.shape[1]
    tm = _tile(PROJ_ROWS, n)
    tn = _tile(PROJ_COLS, p)
    return pl.pallas_call(
        _gate_kernel,
        grid=(n // tm, p // tn),
        in_specs=[pl.BlockSpec((tm, k), lambda i, j: (i, 0)),
                  pl.BlockSpec((k, tn), lambda i, j: (0, j))],
        out_specs=pl.BlockSpec((tm, tn), lambda i, j: (i, j)),
        out_shape=jax.ShapeDtypeStruct((n, p), BF16),
        compiler_params=_params("parallel", "arbitrary"),
        name="gate_proj",
    )(a, w)


def _conv_kernel(a_ref, wb_ref, wc_ref, wu_ref, cw_ref, z_ref, carry_ref, *, tiles_per_seq, conv_k):
    i = pl.program_id(0)
    j = pl.program_id(1)

    @pl.when(i % tiles_per_seq == 0)
    def _():
        carry_ref[j] = jnp.zeros(carry_ref.shape[1:], F32)

    a = a_ref[...]
    b = _dot(a, wb_ref[...])
    cu = _dot(a, wc_ref[...]) * _dot(a, wu_ref[...])
    tm = cu.shape[0]
    prev = carry_ref[j]
    row = lax.broadcasted_iota(I32, cu.shape, 0)
    cw = cw_ref[...]
    y = cu * cw[conv_k - 1:conv_k]
    for s in range(1, conv_k):
        sh = pltpu.roll(cu, s, 0)
        for r in range(s):
            src = SUBLANES - s + r
            sh = jnp.where(row == r, prev[src:src + 1], sh)
        y = y + sh * cw[conv_k - 1 - s:conv_k - s]
    z_ref[...] = (b * y).astype(z_ref.dtype)
    carry_ref[j] = cu[tm - SUBLANES:tm]


def _conv_front(a, w_bcu, conv_w_pad, *, d_conv, seq, conv_k):
    n, k = a.shape
    tm = _tile(PROJ_ROWS, seq)
    tn = _tile(CONV_COLS, d_conv)
    nj = d_conv // tn
    return pl.pallas_call(
        functools.partial(_conv_kernel, tiles_per_seq=seq // tm, conv_k=conv_k),
        grid=(n // tm, nj),
        in_specs=[pl.BlockSpec((tm, k), lambda i, j: (i, 0)),
                  pl.BlockSpec((k, tn), lambda i, j: (0, j)),
                  pl.BlockSpec((k, tn), lambda i, j: (0, nj + j)),
                  pl.BlockSpec((k, tn), lambda i, j: (0, 2 * nj + j)),
                  pl.BlockSpec((SUBLANES, tn), lambda i, j: (0, j))],
        out_specs=pl.BlockSpec((tm, tn), lambda i, j: (i, j)),
        out_shape=jax.ShapeDtypeStruct((n, d_conv), BF16),
        scratch_shapes=[pltpu.VMEM((nj, SUBLANES, tn), F32)],
        compiler_params=_params("arbitrary", "arbitrary"),
        name="conv_front",
    )(a, w_bcu, w_bcu, w_bcu, conv_w_pad)


def _qkv_kernel(a_ref, w_ref, wf_ref, bf_ref, o_ref, c_ref, carry_ref, *, tiles_per_seq):
    i = pl.program_id(0)
    j = pl.program_id(1)
    a = a_ref[...]
    o_ref[...] = _dot(a, w_ref[...]).astype(o_ref.dtype)

    @pl.when(j == 0)
    def _():
        @pl.when(i % tiles_per_seq == 0)
        def _():
            carry_ref[...] = jnp.zeros(carry_ref.shape, F32)

        lf = jax.nn.log_sigmoid(_dot_nt(wf_ref[...], a) + bf_ref[...])
        tm = lf.shape[1]
        r = lax.broadcasted_iota(I32, (tm, tm), 0)
        c = lax.broadcasted_iota(I32, (tm, tm), 1)
        upper = jnp.where(r <= c, 1.0, 0.0).astype(BF16)
        cs = _dot_f32_by_01(lf, upper) + carry_ref[:, 0:1]
        c_ref[0] = cs * LOG2_E
        carry_ref[...] = jnp.broadcast_to(cs[:, tm - 1:tm], carry_ref.shape)


def _qkv_proj(a, w, wf_t, bf_col, *, batch, seq):
    n, k = a.shape
    p = w.shape[1]
    hp = wf_t.shape[0]
    tm = _tile(PROJ_ROWS, seq)
    tn = _tile(PROJ_COLS, p)
    ns = seq // tm
    return pl.pallas_call(
        functools.partial(_qkv_kernel, tiles_per_seq=ns),
        grid=(n // tm, p // tn),
        in_specs=[pl.BlockSpec((tm, k), lambda i, j: (i, 0)),
                  pl.BlockSpec((k, tn), lambda i, j: (0, j)),
                  pl.BlockSpec((hp, k), lambda i, j: (0, 0)),
                  pl.BlockSpec((hp, 1), lambda i, j: (0, 0))],
        out_specs=[pl.BlockSpec((tm, tn), lambda i, j: (i, j)),
                   pl.BlockSpec((1, hp, tm), lambda i, j: (i // ns, 0, i % ns))],
        out_shape=[jax.ShapeDtypeStruct((n, p), BF16),
                   jax.ShapeDtypeStruct((batch, hp, seq), F32)],
        scratch_shapes=[pltpu.VMEM((hp, LANES), F32)],
        compiler_params=_params("arbitrary", "arbitrary"),
        name="qkv_proj",
    )(a, w, wf_t, bf_col)


def _attn_kernel(q_ref, k_ref, v_ref, c_ref, o_ref, s_ref, p_ref, al_ref, m_ref, acc_ref, *, t):
    qi = pl.program_id(2)
    dh = q_ref.shape[1]
    reps = t // LANES
    q0 = pl.multiple_of(qi * t, t)
    c0 = c_ref[0, 0, :, pl.ds(q0, LANES)][:, 0:1]
    ones_col = jnp.where(lax.broadcasted_iota(I32, (t, LANES), 1) == 0, 1.0, 0.0).astype(BF16)

    def score(slot, k0, masked):
        s = _dot_nt(q_ref[...], k_ref[pl.ds(k0, t), :]) - (c_ref[0, 0, :, pl.ds(k0, t)] - c0)
        if masked:
            row = lax.broadcasted_iota(I32, (t, t), 0)
            col = lax.broadcasted_iota(I32, (t, t), 1)
            s = jnp.where(col <= row, s, -jnp.inf)
        s_ref[slot] = s

    def softmax(slot):
        s = s_ref[slot]
        m_prev = m_ref[...]
        m_new = jnp.maximum(m_prev, jnp.max(s, axis=-1, keepdims=True))
        al_ref[slot] = jnp.exp2(m_prev - m_new)
        p_ref[slot] = jnp.exp2(s - jnp.concatenate([m_new] * reps, axis=1)).astype(BF16)
        m_ref[...] = m_new

    def apply(slot, k0):
        vaug = jnp.concatenate([v_ref[pl.ds(k0, t), :], ones_col], axis=1)
        al = jnp.concatenate([al_ref[slot]] * (acc_ref.shape[1] // LANES), axis=1)
        acc_ref[...] = al * acc_ref[...] + _dot(p_ref[slot], vaug)

    def key_start(item):
        return pl.multiple_of(jnp.where(item == 0, q0, jnp.maximum(item - 1, 0) * t), t)

    m_ref[...] = jnp.full(m_ref.shape, -jnp.inf, F32)
    acc_ref[...] = jnp.zeros(acc_ref.shape, F32)
    p_ref[1] = jnp.zeros(p_ref.shape[1:], BF16)
    al_ref[1] = jnp.ones(al_ref.shape[1:], F32)
    score(0, q0, True)

    def step(i, slot):
        score(1 - slot, pl.multiple_of(i * t, t), False)
        softmax(slot)
        apply(1 - slot, key_start(i - 1))

    def drain(last, slot):
        softmax(slot)
        apply(1 - slot, key_start(last - 1))
        apply(slot, key_start(last))

    def body(g, carry):
        for u in range(ATTN_UNROLL):
            step(g * ATTN_UNROLL + u, u % 2)
        return carry

    last = qi
    groups = last // ATTN_UNROLL
    lax.fori_loop(0, groups, body, 0)
    base = groups * ATTN_UNROLL
    for rem in range(ATTN_UNROLL):
        @pl.when(last - base == rem)
        def _(rem=rem):
            for u in range(rem):
                step(base + u, u % 2)
            drain(last, rem % 2)

    acc = acc_ref[...]
    o_ref[...] = (acc[:, :dh] / acc[:, dh:dh + 1]).astype(o_ref.dtype)


def _attention(qkv, c, *, batch, seq, n_heads, head_dim):
    n = qkv.shape[0]
    t = _tile(ATTN_TILE, seq)
    nq = seq // t
    c = c.reshape(batch, c.shape[1], 1, seq)
    return pl.pallas_call(
        functools.partial(_attn_kernel, t=t),
        grid=(batch, n_heads, nq),
        in_specs=[pl.BlockSpec((t, head_dim), lambda b, h, i: (b * nq + i, h)),
                  pl.BlockSpec((seq, head_dim), lambda b, h, i: (b, n_heads + h)),
                  pl.BlockSpec((seq, head_dim), lambda b, h, i: (b, 2 * n_heads + h)),
                  pl.BlockSpec((1, 1, 1, seq), lambda b, h, i: (b, h, 0, 0))],
        out_specs=pl.BlockSpec((t, head_dim), lambda b, h, i: (b * nq + i, h)),
        out_shape=jax.ShapeDtypeStruct((n, n_heads * head_dim), BF16),
        scratch_shapes=[pltpu.VMEM((2, t, t), F32), pltpu.VMEM((2, t, t), BF16),
                        pltpu.VMEM((2, t, LANES), F32), pltpu.VMEM((t, LANES), F32),
                        pltpu.VMEM((t, head_dim + LANES), F32)],
        compiler_params=_params("parallel", "parallel", "arbitrary"),
        name="forget_attn",
    )(qkv, qkv, qkv, c)


def _mix_kernel(z_ref, o_ref, ga_ref, gb_ref, h_ref, wc_ref, wa_ref, wo_ref, g_ref, hn_ref, a_ref, *, packed):
    yc = _dot(z_ref[...], wc_ref[...])
    ya = _dot(o_ref[...], wa_ref[...])
    m = ga_ref[...].astype(F32) * yc + gb_ref[...].astype(F32) * ya
    hn = h_ref[...] + _dot(m.astype(BF16), wo_ref[...])
    hn_ref[...] = hn
    a = _rms(hn, g_ref[...])
    if packed:
        _pack_rows(a_ref, a)
    else:
        a_ref[...] = a.astype(a_ref.dtype)


def _mix_out(z, o, gates, h, wc, wa, wo, g, *, packed):
    n, d = h.shape
    dc = z.shape[1]
    da = o.shape[1]
    tm = _tile(MIX_ROWS, n)
    const = lambda i: (0, 0)
    if packed:
        a_spec = pl.BlockSpec((tm * SUBLANES, LANES), lambda i: (i, 0))
        a_shape = jax.ShapeDtypeStruct((n * SUBLANES, LANES), U32)
    else:
        a_spec = pl.BlockSpec((tm, d), lambda i: (i, 0))
        a_shape = jax.ShapeDtypeStruct((n, d), BF16)
    return pl.pallas_call(
        functools.partial(_mix_kernel, packed=packed),
        grid=(n // tm,),
        in_specs=[pl.BlockSpec((tm, dc), lambda i: (i, 0)),
                  pl.BlockSpec((tm, da), lambda i: (i, 0)),
                  pl.BlockSpec((tm, d), lambda i: (i, 0)),
                  pl.BlockSpec((tm, d), lambda i: (i, 1)),
                  pl.BlockSpec((tm, d), lambda i: (i, 0)),
                  pl.BlockSpec((dc, d), const, pipeline_mode=pl.Buffered(1)),
                  pl.BlockSpec((da, d), const, pipeline_mode=pl.Buffered(1)),
                  pl.BlockSpec((d, d), const, pipeline_mode=pl.Buffered(1)),
                  pl.BlockSpec((1, d), const)],
        out_specs=[pl.BlockSpec((tm, d), lambda i: (i, 0)), a_spec],
        out_shape=[jax.ShapeDtypeStruct((n, d), F32), a_shape],
        compiler_params=_params("parallel"),
        name="mix_out",
    )(z, o, gates, gates, h, wc, wa, wo, g)


def _epilogue_shapes(n, d, final):
    if final:
        return [jax.ShapeDtypeStruct((n, d), F32)]
    return [jax.ShapeDtypeStruct((n, d), F32), jax.ShapeDtypeStruct((n, d), BF16)]


def _write_epilogue(hn, g, outs, final):
    if final:
        outs[0][...] = _rms(hn, g)
    else:
        outs[0][...] = hn
        outs[1][...] = _rms(hn, g).astype(outs[1].dtype)


def _ffn_kernel(a_ref, wg_ref, wu_ref, wd_ref, h_ref, g_ref, *rest, final):
    outs, acc_ref = rest[:-1], rest[-1]
    f = pl.program_id(1)

    @pl.when(f == 0)
    def _():
        acc_ref[...] = h_ref[...]

    a = a_ref[...]
    hid = (jax.nn.silu(_dot(a, wg_ref[...])) * _dot(a, wu_ref[...])).astype(BF16)
    acc_ref[...] += _dot(hid, wd_ref[...])

    @pl.when(f == pl.num_programs(1) - 1)
    def _():
        _write_epilogue(acc_ref[...], g_ref[...], outs, final)


def _dense_ffn(a, h, wg, wu, wd, g, *, final):
    n, d = h.shape
    ff = wg.shape[1]
    tm = _tile(FFN_ROWS, n)
    tf = _tile(FFN_COLS, ff)
    row = lambda i, f: (i, 0)
    shapes = _epilogue_shapes(n, d, final)
    return pl.pallas_call(
        functools.partial(_ffn_kernel, final=final),
        grid=(n // tm, ff // tf),
        in_specs=[pl.BlockSpec((tm, d), row),
                  pl.BlockSpec((d, tf), lambda i, f: (0, f)),
                  pl.BlockSpec((d, tf), lambda i, f: (0, f)),
                  pl.BlockSpec((tf, d), lambda i, f: (f, 0)),
                  pl.BlockSpec((tm, d), row),
                  pl.BlockSpec((1, d), lambda i, f: (0, 0))],
        out_specs=[pl.BlockSpec((tm, d), row) for _ in shapes],
        out_shape=shapes,
        scratch_shapes=[pltpu.VMEM((tm, d), F32)],
        compiler_params=_params("parallel", "arbitrary"),
        name="dense_ffn",
    )(a, wg, wu, wd, h, g)


def _router_kernel(h_ref, g_ref, rw_ref, rb_ref, rc_ref, cnt_ref, carry_ref, *, n_exp):
    i = pl.program_id(0)

    @pl.when(i == 0)
    def _():
        carry_ref[...] = jnp.zeros(carry_ref.shape, F32)

    a = _rms(h_ref[...], g_ref[...])
    tm = a.shape[0]
    lane = lax.broadcasted_iota(I32, (tm, LANES), 1)
    lanef = lane.astype(F32)
    logits = jnp.where(lane < n_exp, _dot_f32(a, rw_ref[...]) + rb_ref[...], -jnp.inf)
    big = float(LANES)
    m1 = jnp.max(logits, axis=-1, keepdims=True)
    i1 = jnp.min(jnp.where(logits == m1, lanef, big), axis=-1, keepdims=True)
    rest = jnp.where(lanef == i1, -jnp.inf, logits)
    m2 = jnp.max(rest, axis=-1, keepdims=True)
    i2 = jnp.min(jnp.where(rest == m2, lanef, big), axis=-1, keepdims=True)
    e2 = jnp.exp(m2 - m1)
    w1 = 1.0 / (1.0 + e2)
    w2 = e2 / (1.0 + e2)

    onehot = jnp.where(lanef == i1, 1.0, jnp.where(lanef == i2, 1.0, 0.0))
    r = lax.broadcasted_iota(I32, (tm, tm), 0)
    c = lax.broadcasted_iota(I32, (tm, tm), 1)
    strict_lower = jnp.where(c < r, 1.0, 0.0).astype(BF16)
    before = carry_ref[0:1, :]
    prefix = _dot(strict_lower, onehot.astype(BF16)) + before
    rank1 = jnp.sum(jnp.where(lanef == i1, prefix, 0.0), axis=-1, keepdims=True)
    rank2 = jnp.sum(jnp.where(lanef == i2, prefix, 0.0), axis=-1, keepdims=True)
    after = before + jnp.sum(onehot, axis=0, keepdims=True)

    srow = lax.broadcasted_iota(I32, (SUBLANES, LANES), 0)
    cnt_ref[0] = jnp.where(srow == 0, before, jnp.where(srow == 1, after, 0.0))
    carry_ref[...] = jnp.broadcast_to(after, carry_ref.shape)

    rec = jnp.where(lane == 0, i1, jnp.where(lane == 1, i2, jnp.where(
        lane == 2, rank1, jnp.where(lane == 3, rank2, jnp.where(
            lane == 4, w1, jnp.where(lane == 5, w2, 0.0))))))
    rc_ref[...] = rec


def _router(h, g, rw_pad, rb_pad, *, n_exp):
    n, d = h.shape
    tm = _tile(MOE_CHUNK, n)
    nc = n // tm
    return pl.pallas_call(
        functools.partial(_router_kernel, n_exp=n_exp),
        grid=(nc,),
        in_specs=[pl.BlockSpec((tm, d), lambda i: (i, 0)),
                  pl.BlockSpec((1, d), lambda i: (0, 0)),
                  pl.BlockSpec((d, LANES), lambda i: (0, 0)),
                  pl.BlockSpec((1, LANES), lambda i: (0, 0))],
        out_specs=[pl.BlockSpec((tm, LANES), lambda i: (i, 0)),
                   pl.BlockSpec((1, SUBLANES, LANES), lambda i: (i, 0, 0))],
        out_shape=[jax.ShapeDtypeStruct((n, LANES), F32),
                   jax.ShapeDtypeStruct((nc, SUBLANES, LANES), F32)],
        scratch_shapes=[pltpu.VMEM((SUBLANES, LANES), F32)],
        compiler_params=_params("arbitrary"),
        name="moe_router",
    )(h, g, rw_pad, rb_pad)


def _moe_plan(rc, cnt, *, n_exp, n_tiles):
    counts = cnt[-1, 1, :n_exp].astype(I32)
    tiles_e = (counts + MOE_TILE - 1) // MOE_TILE
    tile_end = jnp.cumsum(tiles_e)
    n_used = tile_end[-1]
    pad_start = (tile_end - tiles_e) * MOE_TILE
    t_ids = jnp.arange(n_tiles, dtype=I32)
    tile_e = jnp.sum(jnp.minimum(t_ids, n_used - 1)[:, None] >= tile_end[None, :], axis=1).astype(I32)
    rec = rc[:, :2 * TOP_K].astype(I32)
    start_of = lambda e: jnp.sum(jnp.where(e[:, None] == jnp.arange(n_exp, dtype=I32)[None, :], pad_start[None, :], 0), axis=1)
    dest1 = start_of(rec[:, 0]) + rec[:, 2]
    dest2 = start_of(rec[:, 1]) + rec[:, 3]
    return dict(tile_e=tile_e, n_used=n_used.reshape(1).astype(I32), dest1=dest1.astype(I32), dest2=dest2.astype(I32))


def _row_tile(ref, row):
    return ref.at[pl.ds(pl.multiple_of(row * SUBLANES, SUBLANES), SUBLANES), :]


def _wait_rows(src_ref, dst_ref, sem, n_rows):
    n = n_rows * SUBLANES
    pltpu.make_async_copy(src_ref.at[pl.ds(0, n), :], dst_ref.at[pl.ds(0, n), :], sem).wait()


def _dispatch_kernel(d1_ref, d2_ref, a_ref, zeros_hbm, xb_hbm, sem):
    del zeros_hbm
    chunk = a_ref.shape[0] // SUBLANES
    base = pl.program_id(0) * chunk

    def body(i, carry):
        t = base + i
        src = _row_tile(a_ref, i)
        pltpu.make_async_copy(src, _row_tile(xb_hbm, d1_ref[t]), sem).start()
        pltpu.make_async_copy(src, _row_tile(xb_hbm, d2_ref[t]), sem).start()
        return carry

    lax.fori_loop(0, chunk, body, 0)
    for _ in range(TOP_K):
        _wait_rows(a_ref, xb_hbm, sem, chunk)


def _moe_dispatch(plan, a_packed, *, n_rows):
    n = a_packed.shape[0] // SUBLANES
    chunk = _tile(MOE_CHUNK, n)
    any_spec = pl.BlockSpec(memory_space=pl.ANY)
    spec = pltpu.PrefetchScalarGridSpec(
        num_scalar_prefetch=2,
        grid=(n // chunk,),
        in_specs=[pl.BlockSpec((chunk * SUBLANES, LANES), lambda c, d1, d2: (c, 0)), any_spec],
        out_specs=any_spec,
        scratch_shapes=[pltpu.SemaphoreType.DMA(())],
    )
    zeros = jnp.zeros((n_rows * SUBLANES, LANES), U32)
    return pl.pallas_call(
        _dispatch_kernel,
        grid_spec=spec,
        out_shape=jax.ShapeDtypeStruct(zeros.shape, U32),
        input_output_aliases={3: 0},
        compiler_params=_params("arbitrary"),
        name="moe_dispatch",
    )(plan["dest1"], plan["dest2"], a_packed, zeros)


def _expert_kernel(te_ref, nu_ref, x_ref, wg_ref, wu_ref, wd_ref, y_ref, acc_ref):
    t = pl.program_id(0)
    f = pl.program_id(1)
    rows = acc_ref.shape[0]

    @pl.when(t < nu_ref[0])
    def _():
        @pl.when(f == 0)
        def _():
            acc_ref[...] = jnp.zeros(acc_ref.shape, F32)

        lo, hi = _unpack_rows(x_ref, rows)
        x = jnp.concatenate([lo.astype(BF16), hi.astype(BF16)], axis=1)
        hid = (jax.nn.silu(_dot(x, wg_ref[0])) * _dot(x, wu_ref[0])).astype(BF16)
        acc_ref[...] += _dot(hid, wd_ref[0])

        @pl.when(f == pl.num_programs(1) - 1)
        def _():
            _pack_rows(y_ref, acc_ref[...])

    @pl.when((t >= nu_ref[0]) & (f == pl.num_programs(1) - 1))
    def _():
        y_ref[...] = jnp.zeros(y_ref.shape, y_ref.dtype)


def _moe_experts(plan, xb, wg, wu, wd):
    d, ff = wg.shape[1], wg.shape[2]
    tile = MOE_TILE
    n_tiles = xb.shape[0] // (tile * SUBLANES)
    tf = _tile(MOE_COLS, ff)
    nf = ff // tf

    def fcol(t, f, nu):
        return jnp.where(t < nu[0], f, nf - 1)

    spec = pltpu.PrefetchScalarGridSpec(
        num_scalar_prefetch=2,
        grid=(n_tiles, nf),
        in_specs=[pl.BlockSpec((tile * SUBLANES, LANES), lambda t, f, te, nu: (jnp.minimum(t, nu[0] - 1), 0)),
                  pl.BlockSpec((1, d, tf), lambda t, f, te, nu: (te[t], 0, fcol(t, f, nu))),
                  pl.BlockSpec((1, d, tf), lambda t, f, te, nu: (te[t], 0, fcol(t, f, nu))),
                  pl.BlockSpec((1, tf, d), lambda t, f, te, nu: (te[t], fcol(t, f, nu), 0))],
        out_specs=pl.BlockSpec((tile * SUBLANES, LANES), lambda t, f, te, nu: (t, 0)),
        scratch_shapes=[pltpu.VMEM((tile, d), F32)],
    )
    return pl.pallas_call(
        _expert_kernel,
        grid_spec=spec,
        out_shape=jax.ShapeDtypeStruct(xb.shape, U32),
        compiler_params=_params("arbitrary", "arbitrary"),
        name="moe_experts",
    )(plan["tile_e"], plan["n_used"], xb, wg, wu, wd)


def _combine_kernel(d1_ref, d2_ref, rc_ref, h_ref, g_ref, y_hbm, *rest, final):
    outs, (y1_ref, y2_ref, sem) = rest[:-3], rest[-3:]
    chunk = h_ref.shape[0]
    base = pl.program_id(0) * chunk

    def body(i, carry):
        t = base + i
        pltpu.make_async_copy(_row_tile(y_hbm, d1_ref[t]), _row_tile(y1_ref, i), sem).start()
        pltpu.make_async_copy(_row_tile(y_hbm, d2_ref[t]), _row_tile(y2_ref, i), sem).start()
        return carry

    lax.fori_loop(0, chunk, body, 0)
    _wait_rows(y_hbm, y1_ref, sem, chunk)
    _wait_rows(y_hbm, y2_ref, sem, chunk)
    rc = rc_ref[...]
    w1, w2 = rc[:, 4:5], rc[:, 5:6]
    lo1, hi1 = _unpack_rows(y1_ref, chunk)
    lo2, hi2 = _unpack_rows(y2_ref, chunk)
    y = jnp.concatenate([w1 * lo1 + w2 * lo2, w1 * hi1 + w2 * hi2], axis=1)
    _write_epilogue(h_ref[...] + y, g_ref[...], outs, final)


def _moe_combine(plan, rc, yb, h, g, *, final):
    n, d = h.shape
    chunk = _tile(MOE_CHUNK, n)
    tok = lambda c, d1, d2: (c, 0)
    shapes = _epilogue_shapes(n, d, final)
    spec = pltpu.PrefetchScalarGridSpec(
        num_scalar_prefetch=2,
        grid=(n // chunk,),
        in_specs=[pl.BlockSpec((chunk, LANES), tok),
                  pl.BlockSpec((chunk, d), tok),
                  pl.BlockSpec((1, d), lambda c, d1, d2: (0, 0)),
                  pl.BlockSpec(memory_space=pl.ANY)],
        out_specs=[pl.BlockSpec((chunk, d), tok) for _ in shapes],
        scratch_shapes=[pltpu.VMEM((chunk * SUBLANES, LANES), U32),
                        pltpu.VMEM((chunk * SUBLANES, LANES), U32),
                        pltpu.SemaphoreType.DMA(())],
    )
    return pl.pallas_call(
        functools.partial(_combine_kernel, final=final),
        grid_spec=spec,
        out_shape=shapes,
        compiler_params=_params("arbitrary"),
        name="moe_combine",
    )(plan["dest1"], plan["dest2"], rc, h, g, yb)


def _moe_ffn(a_packed, h, g_ffn, g_next, rw, rb, wg, wu, wd, *, final):
    n, d = h.shape
    n_exp = rw.shape[1]
    assert n_exp <= LANES and TOP_K == 2
    rw_pad = jnp.zeros((d, LANES), F32).at[:, :n_exp].set(rw)
    rb_pad = jnp.zeros((1, LANES), F32).at[0, :n_exp].set(rb)
    rc, cnt = _router(h, g_ffn, rw_pad, rb_pad, n_exp=n_exp)
    n_tiles = (n * TOP_K) // MOE_TILE + n_exp
    plan = _moe_plan(rc, cnt, n_exp=n_exp, n_tiles=n_tiles)
    xb = _moe_dispatch(plan, a_packed, n_rows=n_tiles * MOE_TILE)
    yb = _moe_experts(plan, xb, wg, wu, wd)
    return _moe_combine(plan, rc, yb, h, g_next, final=final)


def kernel(x, mix_norm, w_in, b_forget, conv_w, w_conv_out, w_attn_out, w_o, ffn_norm, dense_w_gate, dense_w_up, dense_w_down, router_w, router_b, moe_w_gate, moe_w_up, moe_w_down, final_norm):
    batch, seq, d = x.shape
    depth = mix_norm.shape[0]
    conv_k, d_conv = conv_w.shape[1], conv_w.shape[2]
    d_attn = w_attn_out.shape[1]
    n_heads = b_forget.shape[1]
    head_dim = d_attn // n_heads
    assert head_dim % LANES == 0 and conv_k - 1 <= SUBLANES
    n = batch * seq
    hp = 2 * SUBLANES
    assert n_heads <= hp
    scale = LOG2_E / math.sqrt(head_dim)
    o_q = 3 * d_conv
    o_f = o_q + 3 * d_attn
    o_g = o_f + n_heads

    h = x.reshape(n, d)
    a = _norm(h, mix_norm[0].reshape(1, d))
    out = None
    for layer in range(depth):
        wl = w_in[layer]
        w_bcu = wl[:, :o_q].astype(BF16)
        w_qkv = jnp.concatenate([wl[:, o_q:o_q + d_attn] * scale, wl[:, o_q + d_attn:o_f]], axis=1).astype(BF16)
        wf_t = jnp.zeros((hp, d), F32).at[:n_heads].set(wl[:, o_f:o_g].T).astype(BF16)
        bf_col = jnp.zeros((hp, 1), F32).at[:n_heads, 0].set(b_forget[layer])
        w_gates = wl[:, o_g:].astype(BF16)
        cw_pad = jnp.zeros((SUBLANES, d_conv), F32).at[:conv_k].set(conv_w[layer])

        z = _conv_front(a, w_bcu, cw_pad, d_conv=d_conv, seq=seq, conv_k=conv_k)
        qkv, c = _qkv_proj(a, w_qkv, wf_t, bf_col, batch=batch, seq=seq)
        gates = _gate_proj(a, w_gates)
        o = _attention(qkv, c, batch=batch, seq=seq, n_heads=n_heads, head_dim=head_dim)
        h, a = _mix_out(z, o, gates, h, w_conv_out[layer].astype(BF16), w_attn_out[layer].astype(BF16),
                        w_o[layer].astype(BF16), ffn_norm[layer].reshape(1, d), packed=layer % 2 == 1)

        final = layer == depth - 1
        g_next = (final_norm if final else mix_norm[layer + 1]).reshape(1, d)
        i = layer // 2
        if layer % 2 == 0:
            res = _dense_ffn(a, h, dense_w_gate[i].astype(BF16), dense_w_up[i].astype(BF16),
                             dense_w_down[i].astype(BF16), g_next, final=final)
        else:
            res = _moe_ffn(a, h, ffn_norm[layer].reshape(1, d), g_next, router_w[i], router_b[i],
                           moe_w_gate[i].astype(BF16), moe_w_up[i].astype(BF16),
                           moe_w_down[i].astype(BF16), final=final)
        if final:
            out = res[0]
        else:
            h, a = res
    return out.reshape(batch, seq, d)
```

```python
import functools
import math

import jax
import jax.numpy as jnp
from jax import lax
from jax.experimental import pallas as pl
from jax.experimental.pallas import tpu as pltpu

F32 = jnp.float32
BF16 = jnp.bfloat16
I32 = jnp.int32
U32 = jnp.uint32

RMS_EPS = 1e-6
LOG2_E = math.log2(math.e)
TOP_K = 2
LANES = 128
SUBLANES = 8
VMEM_LIMIT = 60 * 1024 * 1024

PROJ_ROWS = 1024
PROJ_COLS = 1024
CONV_COLS = 512
ATTN_TILE = 512
ATTN_KEYS = 256
ATTN_UNROLL = 16
MIX_ROWS = 512
FFN_ROWS = 512
FFN_COLS = 512
MOE_CHUNK = 512
MOE_TILE = 1024
MOE_COLS = 1024


def _tile(pref, dim):
    t = min(pref, dim)
    assert dim % t == 0, (pref, dim)
    return t


def _params(*sem):
    return pltpu.CompilerParams(dimension_semantics=sem, vmem_limit_bytes=VMEM_LIMIT)


def _dot(a, b):
    return jnp.dot(a, b, preferred_element_type=F32)


def _dot_nt(a, b):
    return lax.dot_general(a, b, (((1,), (1,)), ((), ())), preferred_element_type=F32)


def _split3(x):
    hi = x.astype(BF16)
    r = x - hi.astype(F32)
    mid = r.astype(BF16)
    lo = (r - mid.astype(F32)).astype(BF16)
    return hi, mid, lo


def _dot_f32_by_01(x, onehot):
    hi, mid, lo = _split3(x)
    return _dot(hi, onehot) + _dot(mid, onehot) + _dot(lo, onehot)


def _dot_f32(a, b):
    a1, a2, a3 = _split3(a)
    b1, b2, b3 = _split3(b)
    small = _dot(a1, b3) + _dot(a3, b1) + _dot(a2, b2)
    mid = _dot(a1, b2) + _dot(a2, b1)
    return _dot(a1, b1) + (mid + small)


def _rms(xf, g):
    ms = jnp.mean(xf * xf, axis=-1, keepdims=True)
    return xf * lax.rsqrt(ms + RMS_EPS) * g


def _pack_rows(ref, x):
    m, d = x.shape
    half = d // 2
    assert half == SUBLANES * LANES
    lo = lax.bitcast_convert_type(x[:, :half].astype(BF16).astype(F32), U32) >> 16
    hi = lax.bitcast_convert_type(x[:, half:].astype(BF16).astype(F32), U32) & jnp.uint32(0xFFFF0000)
    w = hi | lo
    for r in range(SUBLANES):
        ref[pl.ds(r, m, stride=SUBLANES), :] = w[:, r * LANES:(r + 1) * LANES]


def _unpack_rows(ref, m):
    w = jnp.concatenate([ref[pl.ds(r, m, stride=SUBLANES), :] for r in range(SUBLANES)], axis=1)
    lo = lax.bitcast_convert_type(w << 16, F32)
    hi = lax.bitcast_convert_type(w & jnp.uint32(0xFFFF0000), F32)
    return lo, hi


def _norm_kernel(x_ref, g_ref, a_ref):
    a_ref[...] = _rms(x_ref[...], g_ref[...]).astype(a_ref.dtype)


def _norm(x, g):
    n, d = x.shape
    tm = _tile(FFN_ROWS, n)
    return pl.pallas_call(
        _norm_kernel,
        grid=(n // tm,),
        in_specs=[pl.BlockSpec((tm, d), lambda i: (i, 0)),
                  pl.BlockSpec((1, d), lambda i: (0, 0))],
        out_specs=pl.BlockSpec((tm, d), lambda i: (i, 0)),
        out_shape=jax.ShapeDtypeStruct((n, d), BF16),
        compiler_params=_params("parallel"),
        name="rmsnorm",
    )(x, g)


def _gate_kernel(a_ref, w_ref, o_ref):
    o_ref[...] = jax.nn.sigmoid(_dot(a_ref[...], w_ref[...])).astype(o_ref.dtype)


def _gate_proj(a, w):
    n, k = a.shape
    p = w.shape[1]
    tm = _tile(PROJ_ROWS, n)
    tn = _tile(PROJ_COLS, p)
    return pl.pallas_call(
        _gate_kernel,
        grid=(n // tm, p // tn),
        in_specs=[pl.BlockSpec((tm, k), lambda i, j: (i, 0)),
                  pl.BlockSpec((k, tn), lambda i, j: (0, j))],
        out_specs=pl.BlockSpec((tm, tn), lambda i, j: (i, j)),
        out_shape=jax.ShapeDtypeStruct((n, p), BF16),
        compiler_params=_params("parallel", "arbitrary"),
        name="gate_proj",
    )(a, w)


def _conv_kernel(a_ref, wb_ref, wc_ref, wu_ref, cw_ref, z_ref, carry_ref, *, tiles_per_seq, conv_k):
    i = pl.program_id(0)
    j = pl.program_id(1)

    @pl.when(i % tiles_per_seq == 0)
    def _():
        carry_ref[j] = jnp.zeros(carry_ref.shape[1:], F32)

    a = a_ref[...]
    b = _dot(a, wb_ref[...])
    cu = _dot(a, wc_ref[...]) * _dot(a, wu_ref[...])
    tm = cu.shape[0]
    prev = carry_ref[j]
    row = lax.broadcasted_iota(I32, cu.shape, 0)
    cw = cw_ref[...]
    y = cu * cw[conv_k - 1:conv_k]
    for s in range(1, conv_k):
        sh = pltpu.roll(cu, s, 0)
        for r in range(s):
            src = SUBLANES - s + r
            sh = jnp.where(row == r, prev[src:src + 1], sh)
        y = y + sh * cw[conv_k - 1 - s:conv_k - s]
    z_ref[...] = (b * y).astype(z_ref.dtype)
    carry_ref[j] = cu[tm - SUBLANES:tm]


def _conv_front(a, w_bcu, conv_w_pad, *, d_conv, seq, conv_k):
    n, k = a.shape
    tm = _tile(PROJ_ROWS, seq)
    tn = _tile(CONV_COLS, d_conv)
    nj = d_conv // tn
    return pl.pallas_call(
        functools.partial(_conv_kernel, tiles_per_seq=seq // tm, conv_k=conv_k),
        grid=(n // tm, nj),
        in_specs=[pl.BlockSpec((tm, k), lambda i, j: (i, 0)),
                  pl.BlockSpec((k, tn), lambda i, j: (0, j)),
                  pl.BlockSpec((k, tn), lambda i, j: (0, nj + j)),
                  pl.BlockSpec((k, tn), lambda i, j: (0, 2 * nj + j)),
                  pl.BlockSpec((SUBLANES, tn), lambda i, j: (0, j))],
        out_specs=pl.BlockSpec((tm, tn), lambda i, j: (i, j)),
        out_shape=jax.ShapeDtypeStruct((n, d_conv), BF16),
        scratch_shapes=[pltpu.VMEM((nj, SUBLANES, tn), F32)],
        compiler_params=_params("arbitrary", "arbitrary"),
        name="conv_front",
    )(a, w_bcu, w_bcu, w_bcu, conv_w_pad)


def _qkv_kernel(a_ref, w_ref, wf_ref, bf_ref, o_ref, c_ref, carry_ref, *, tiles_per_seq):
    i = pl.program_id(0)
    j = pl.program_id(1)
    a = a_ref[...]
    o_ref[...] = _dot(a, w_ref[...]).astype(o_ref.dtype)

    @pl.when(j == 0)
    def _():
        @pl.when(i % tiles_per_seq == 0)
        def _():
            carry_ref[...] = jnp.zeros(carry_ref.shape, F32)

        lf = jax.nn.log_sigmoid(_dot_nt(wf_ref[...], a) + bf_ref[...])
        tm = lf.shape[1]
        r = lax.broadcasted_iota(I32, (tm, tm), 0)
        c = lax.broadcasted_iota(I32, (tm, tm), 1)
        upper = jnp.where(r <= c, 1.0, 0.0).astype(BF16)
        cs = _dot_f32_by_01(lf, upper) + carry_ref[:, 0:1]
        c_ref[0] = cs * LOG2_E
        carry_ref[...] = jnp.broadcast_to(cs[:, tm - 1:tm], carry_ref.shape)


def _qkv_proj(a, w, wf_t, bf_col, *, batch, seq):
    n, k = a.shape
    p = w.shape[1]
    hp = wf_t.shape[0]
    tm = _tile(PROJ_ROWS, seq)
    tn = _tile(PROJ_COLS, p)
    ns = seq // tm
    return pl.pallas_call(
        functools.partial(_qkv_kernel, tiles_per_seq=ns),
        grid=(n // tm, p // tn),
        in_specs=[pl.BlockSpec((tm, k), lambda i, j: (i, 0)),
                  pl.BlockSpec((k, tn), lambda i, j: (0, j)),
                  pl.BlockSpec((hp, k), lambda i, j: (0, 0)),
                  pl.BlockSpec((hp, 1), lambda i, j: (0, 0))],
        out_specs=[pl.BlockSpec((tm, tn), lambda i, j: (i, j)),
                   pl.BlockSpec((1, hp, tm), lambda i, j: (i // ns, 0, i % ns))],
        out_shape=[jax.ShapeDtypeStruct((n, p), BF16),
                   jax.ShapeDtypeStruct((batch, hp, seq), F32)],
        scratch_shapes=[pltpu.VMEM((hp, LANES), F32)],
        compiler_params=_params("arbitrary", "arbitrary"),
        name="qkv_proj",
    )(a, w, wf_t, bf_col)


def _attn_kernel(q_ref, k_ref, v_ref, c_ref, o_ref, s_ref, p_ref, al_ref, m_ref, acc_ref, *, t, tk):
    qi = pl.program_id(2)
    dh = q_ref.shape[1]
    nd = t // tk
    reps = tk // LANES
    q0 = pl.multiple_of(qi * t, t)
    c0 = c_ref[0, 0, :, pl.ds(q0, LANES)][:, 0:1]
    ones_col = jnp.where(lax.broadcasted_iota(I32, (tk, LANES), 1) == 0, 1.0, 0.0).astype(BF16)

    def key_start(item):
        full = jnp.maximum(item - nd, 0) * tk
        return pl.multiple_of(jnp.where(item < nd, q0 + jnp.maximum(item, 0) * tk, full), tk)

    def score(slot, item, diag):
        k0 = key_start(item)
        s = _dot_nt(q_ref[...], k_ref[pl.ds(k0, tk), :]) - (c_ref[0, 0, :, pl.ds(k0, tk)] - c0)
        if diag is not None:
            row = lax.broadcasted_iota(I32, (t, tk), 0)
            col = lax.broadcasted_iota(I32, (t, tk), 1) + diag * tk
            s = jnp.where(col <= row, s, -jnp.inf)
        s_ref[slot] = s

    def softmax(slot):
        s = s_ref[slot]
        m_prev = m_ref[...]
        m_new = jnp.maximum(m_prev, jnp.max(s, axis=-1, keepdims=True))
        al_ref[slot] = jnp.exp2(m_prev - m_new)
        p_ref[slot] = jnp.exp2(s - jnp.concatenate([m_new] * reps, axis=1)).astype(BF16)
        m_ref[...] = m_new

    def apply(slot, item):
        vaug = jnp.concatenate([v_ref[pl.ds(key_start(item), tk), :], ones_col], axis=1)
        al = jnp.concatenate([al_ref[slot]] * (acc_ref.shape[1] // LANES), axis=1)
        acc_ref[...] = al * acc_ref[...] + _dot(p_ref[slot], vaug)

    def step(i, slot, diag=None):
        score(1 - slot, i + 1, diag)
        softmax(slot)
        apply(1 - slot, i - 1)

    def drain(last, slot):
        softmax(slot)
        apply(1 - slot, last - 1)
        apply(slot, last)

    m_ref[...] = jnp.full(m_ref.shape, -jnp.inf, F32)
    acc_ref[...] = jnp.zeros(acc_ref.shape, F32)
    p_ref[1] = jnp.zeros(p_ref.shape[1:], BF16)
    al_ref[1] = jnp.ones(al_ref.shape[1:], F32)
    score(0, 0, 0)
    for i in range(nd - 1):
        step(i, i % 2, diag=i + 1)

    first = nd - 1
    n_full = qi * nd
    last = first + n_full

    def body(g, carry):
        for u in range(ATTN_UNROLL):
            step(first + g * ATTN_UNROLL + u, (first + u) % 2)
        return carry

    groups = n_full // ATTN_UNROLL
    lax.fori_loop(0, groups, body, 0)
    base = first + groups * ATTN_UNROLL
    for rem in range(ATTN_UNROLL):
        @pl.when(last - base == rem)
        def _(rem=rem):
            for u in range(rem):
                step(base + u, (first + u) % 2)
            drain(last, (first + rem) % 2)

    acc = acc_ref[...]
    o_ref[...] = (acc[:, :dh] / acc[:, dh:dh + 1]).astype(o_ref.dtype)


def _attention(qkv, c, *, batch, seq, n_heads, head_dim):
    n = qkv.shape[0]
    t = _tile(ATTN_TILE, seq)
    tk = _tile(ATTN_KEYS, t)
    assert ATTN_UNROLL % 2 == 0
    nq = seq // t
    c = c.reshape(batch, c.shape[1], 1, seq)
    return pl.pallas_call(
        functools.partial(_attn_kernel, t=t, tk=tk),
        grid=(batch, n_heads, nq),
        in_specs=[pl.BlockSpec((t, head_dim), lambda b, h, i: (b * nq + i, h)),
                  pl.BlockSpec((seq, head_dim), lambda b, h, i: (b, n_heads + h)),
                  pl.BlockSpec((seq, head_dim), lambda b, h, i: (b, 2 * n_heads + h)),
                  pl.BlockSpec((1, 1, 1, seq), lambda b, h, i: (b, h, 0, 0))],
        out_specs=pl.BlockSpec((t, head_dim), lambda b, h, i: (b * nq + i, h)),
        out_shape=jax.ShapeDtypeStruct((n, n_heads * head_dim), BF16),
        scratch_shapes=[pltpu.VMEM((2, t, tk), F32), pltpu.VMEM((2, t, tk), BF16),
                        pltpu.VMEM((2, t, LANES), F32), pltpu.VMEM((t, LANES), F32),
                        pltpu.VMEM((t, head_dim + LANES), F32)],
        compiler_params=_params("parallel", "parallel", "arbitrary"),
        name="forget_attn",
    )(qkv, qkv, qkv, c)


def _mix_kernel(z_ref, o_ref, ga_ref, gb_ref, h_ref, wc_ref, wa_ref, wo_ref, g_ref, hn_ref, a_ref, *, packed):
    yc = _dot(z_ref[...], wc_ref[...])
    ya = _dot(o_ref[...], wa_ref[...])
    m = ga_ref[...].astype(F32) * yc + gb_ref[...].astype(F32) * ya
    hn = h_ref[...] + _dot(m.astype(BF16), wo_ref[...])
    hn_ref[...] = hn
    a = _rms(hn, g_ref[...])
    if packed:
        _pack_rows(a_ref, a)
    else:
        a_ref[...] = a.astype(a_ref.dtype)


def _mix_out(z, o, gates, h, wc, wa, wo, g, *, packed):
    n, d = h.shape
    dc = z.shape[1]
    da = o.shape[1]
    tm = _tile(MIX_ROWS, n)
    const = lambda i: (0, 0)
    if packed:
        a_spec = pl.BlockSpec((tm * SUBLANES, LANES), lambda i: (i, 0))
        a_shape = jax.ShapeDtypeStruct((n * SUBLANES, LANES), U32)
    else:
        a_spec = pl.BlockSpec((tm, d), lambda i: (i, 0))
        a_shape = jax.ShapeDtypeStruct((n, d), BF16)
    return pl.pallas_call(
        functools.partial(_mix_kernel, packed=packed),
        grid=(n // tm,),
        in_specs=[pl.BlockSpec((tm, dc), lambda i: (i, 0)),
                  pl.BlockSpec((tm, da), lambda i: (i, 0)),
                  pl.BlockSpec((tm, d), lambda i: (i, 0)),
                  pl.BlockSpec((tm, d), lambda i: (i, 1)),
                  pl.BlockSpec((tm, d), lambda i: (i, 0)),
                  pl.BlockSpec((dc, d), const, pipeline_mode=pl.Buffered(1)),
                  pl.BlockSpec((da, d), const, pipeline_mode=pl.Buffered(1)),
                  pl.BlockSpec((d, d), const, pipeline_mode=pl.Buffered(1)),
                  pl.BlockSpec((1, d), const)],
        out_specs=[pl.BlockSpec((tm, d), lambda i: (i, 0)), a_spec],
        out_shape=[jax.ShapeDtypeStruct((n, d), F32), a_shape],
        compiler_params=_params("parallel"),
        name="mix_out",
    )(z, o, gates, gates, h, wc, wa, wo, g)


def _epilogue_shapes(n, d, final):
    if final:
        return [jax.ShapeDtypeStruct((n, d), F32)]
    return [jax.ShapeDtypeStruct((n, d), F32), jax.ShapeDtypeStruct((n, d), BF16)]


def _write_epilogue(hn, g, outs, final):
    if final:
        outs[0][...] = _rms(hn, g)
    else:
        outs[0][...] = hn
        outs[1][...] = _rms(hn, g).astype(outs[1].dtype)


def _ffn_kernel(a_ref, wg_ref, wu_ref, wd_ref, h_ref, g_ref, *rest, final):
    outs, acc_ref = rest[:-1], rest[-1]
    f = pl.program_id(1)

    @pl.when(f == 0)
    def _():
        acc_ref[...] = h_ref[...]

    a = a_ref[...]
    hid = (jax.nn.silu(_dot(a, wg_ref[...])) * _dot(a, wu_ref[...])).astype(BF16)
    acc_ref[...] += _dot(hid, wd_ref[...])

    @pl.when(f == pl.num_programs(1) - 1)
    def _():
        _write_epilogue(acc_ref[...], g_ref[...], outs, final)


def _dense_ffn(a, h, wg, wu, wd, g, *, final):
    n, d = h.shape
    ff = wg.shape[1]
    tm = _tile(FFN_ROWS, n)
    tf = _tile(FFN_COLS, ff)
    row = lambda i, f: (i, 0)
    shapes = _epilogue_shapes(n, d, final)
    return pl.pallas_call(
        functools.partial(_ffn_kernel, final=final),
        grid=(n // tm, ff // tf),
        in_specs=[pl.BlockSpec((tm, d), row),
                  pl.BlockSpec((d, tf), lambda i, f: (0, f)),
                  pl.BlockSpec((d, tf), lambda i, f: (0, f)),
                  pl.BlockSpec((tf, d), lambda i, f: (f, 0)),
                  pl.BlockSpec((tm, d), row),
                  pl.BlockSpec((1, d), lambda i, f: (0, 0))],
        out_specs=[pl.BlockSpec((tm, d), row) for _ in shapes],
        out_shape=shapes,
        scratch_shapes=[pltpu.VMEM((tm, d), F32)],
        compiler_params=_params("parallel", "arbitrary"),
        name="dense_ffn",
    )(a, wg, wu, wd, h, g)


def _router_kernel(h_ref, g_ref, rw_ref, rb_ref, rc_ref, cnt_ref, carry_ref, *, n_exp):
    i = pl.program_id(0)

    @pl.when(i == 0)
    def _():
        carry_ref[...] = jnp.zeros(carry_ref.shape, F32)

    a = _rms(h_ref[...], g_ref[...])
    tm = a.shape[0]
    lane = lax.broadcasted_iota(I32, (tm, LANES), 1)
    lanef = lane.astype(F32)
    logits = jnp.where(lane < n_exp, _dot_f32(a, rw_ref[...]) + rb_ref[...], -jnp.inf)
    big = float(LANES)
    m1 = jnp.max(logits, axis=-1, keepdims=True)
    i1 = jnp.min(jnp.where(logits == m1, lanef, big), axis=-1, keepdims=True)
    rest = jnp.where(lanef == i1, -jnp.inf, logits)
    m2 = jnp.max(rest, axis=-1, keepdims=True)
    i2 = jnp.min(jnp.where(rest == m2, lanef, big), axis=-1, keepdims=True)
    e2 = jnp.exp(m2 - m1)
    w1 = 1.0 / (1.0 + e2)
    w2 = e2 / (1.0 + e2)

    onehot = jnp.where(lanef == i1, 1.0, jnp.where(lanef == i2, 1.0, 0.0))
    r = lax.broadcasted_iota(I32, (tm, tm), 0)
    c = lax.broadcasted_iota(I32, (tm, tm), 1)
    strict_lower = jnp.where(c < r, 1.0, 0.0).astype(BF16)
    before = carry_ref[0:1, :]
    prefix = _dot(strict_lower, onehot.astype(BF16)) + before
    rank1 = jnp.sum(jnp.where(lanef == i1, prefix, 0.0), axis=-1, keepdims=True)
    rank2 = jnp.sum(jnp.where(lanef == i2, prefix, 0.0), axis=-1, keepdims=True)
    after = before + jnp.sum(onehot, axis=0, keepdims=True)

    srow = lax.broadcasted_iota(I32, (SUBLANES, LANES), 0)
    cnt_ref[0] = jnp.where(srow == 0, before, jnp.where(srow == 1, after, 0.0))
    carry_ref[...] = jnp.broadcast_to(after, carry_ref.shape)

    rec = jnp.where(lane == 0, i1, jnp.where(lane == 1, i2, jnp.where(
        lane == 2, rank1, jnp.where(lane == 3, rank2, jnp.where(
            lane == 4, w1, jnp.where(lane == 5, w2, 0.0))))))
    rc_ref[...] = rec


def _router(h, g, rw_pad, rb_pad, *, n_exp):
    n, d = h.shape
    tm = _tile(MOE_CHUNK, n)
    nc = n // tm
    return pl.pallas_call(
        functools.partial(_router_kernel, n_exp=n_exp),
        grid=(nc,),
        in_specs=[pl.BlockSpec((tm, d), lambda i: (i, 0)),
                  pl.BlockSpec((1, d), lambda i: (0, 0)),
                  pl.BlockSpec((d, LANES), lambda i: (0, 0)),
                  pl.BlockSpec((1, LANES), lambda i: (0, 0))],
        out_specs=[pl.BlockSpec((tm, LANES), lambda i: (i, 0)),
                   pl.BlockSpec((1, SUBLANES, LANES), lambda i: (i, 0, 0))],
        out_shape=[jax.ShapeDtypeStruct((n, LANES), F32),
                   jax.ShapeDtypeStruct((nc, SUBLANES, LANES), F32)],
        scratch_shapes=[pltpu.VMEM((SUBLANES, LANES), F32)],
        compiler_params=_params("arbitrary"),
        name="moe_router",
    )(h, g, rw_pad, rb_pad)


def _moe_plan(rc, cnt, *, n_exp, n_tiles):
    counts = cnt[-1, 1, :n_exp].astype(I32)
    tiles_e = (counts + MOE_TILE - 1) // MOE_TILE
    tile_end = jnp.cumsum(tiles_e)
    n_used = tile_end[-1]
    pad_start = (tile_end - tiles_e) * MOE_TILE
    t_ids = jnp.arange(n_tiles, dtype=I32)
    tile_e = jnp.sum(jnp.minimum(t_ids, n_used - 1)[:, None] >= tile_end[None, :], axis=1).astype(I32)
    rec = rc[:, :2 * TOP_K].astype(I32)
    start_of = lambda e: jnp.sum(jnp.where(e[:, None] == jnp.arange(n_exp, dtype=I32)[None, :], pad_start[None, :], 0), axis=1)
    dest1 = start_of(rec[:, 0]) + rec[:, 2]
    dest2 = start_of(rec[:, 1]) + rec[:, 3]
    return dict(tile_e=tile_e, n_used=n_used.reshape(1).astype(I32), dest1=dest1.astype(I32), dest2=dest2.astype(I32))


def _row_tile(ref, row):
    return ref.at[pl.ds(pl.multiple_of(row * SUBLANES, SUBLANES), SUBLANES), :]


def _wait_rows(src_ref, dst_ref, sem, n_rows):
    n = n_rows * SUBLANES
    pltpu.make_async_copy(src_ref.at[pl.ds(0, n), :], dst_ref.at[pl.ds(0, n), :], sem).wait()


def _dispatch_kernel(d1_ref, d2_ref, a_ref, zeros_hbm, xb_hbm, sem):
    del zeros_hbm
    chunk = a_ref.shape[0] // SUBLANES
    base = pl.program_id(0) * chunk

    def body(i, carry):
        t = base + i
        src = _row_tile(a_ref, i)
        pltpu.make_async_copy(src, _row_tile(xb_hbm, d1_ref[t]), sem).start()
        pltpu.make_async_copy(src, _row_tile(xb_hbm, d2_ref[t]), sem).start()
        return carry

    lax.fori_loop(0, chunk, body, 0)
    for _ in range(TOP_K):
        _wait_rows(a_ref, xb_hbm, sem, chunk)


def _moe_dispatch(plan, a_packed, *, n_rows):
    n = a_packed.shape[0] // SUBLANES
    chunk = _tile(MOE_CHUNK, n)
    any_spec = pl.BlockSpec(memory_space=pl.ANY)
    spec = pltpu.PrefetchScalarGridSpec(
        num_scalar_prefetch=2,
        grid=(n // chunk,),
        in_specs=[pl.BlockSpec((chunk * SUBLANES, LANES), lambda c, d1, d2: (c, 0)), any_spec],
        out_specs=any_spec,
        scratch_shapes=[pltpu.SemaphoreType.DMA(())],
    )
    zeros = jnp.zeros((n_rows * SUBLANES, LANES), U32)
    return pl.pallas_call(
        _dispatch_kernel,
        grid_spec=spec,
        out_shape=jax.ShapeDtypeStruct(zeros.shape, U32),
        input_output_aliases={3: 0},
        compiler_params=_params("arbitrary"),
        name="moe_dispatch",
    )(plan["dest1"], plan["dest2"], a_packed, zeros)


def _expert_kernel(te_ref, nu_ref, x_ref, wg_ref, wu_ref, wd_ref, y_ref, acc_ref):
    t = pl.program_id(0)
    f = pl.program_id(1)
    rows = acc_ref.shape[0]

    @pl.when(t < nu_ref[0])
    def _():
        @pl.when(f == 0)
        def _():
            acc_ref[...] = jnp.zeros(acc_ref.shape, F32)

        lo, hi = _unpack_rows(x_ref, rows)
        x = jnp.concatenate([lo.astype(BF16), hi.astype(BF16)], axis=1)
        hid = (jax.nn.silu(_dot(x, wg_ref[0])) * _dot(x, wu_ref[0])).astype(BF16)
        acc_ref[...] += _dot(hid, wd_ref[0])

        @pl.when(f == pl.num_programs(1) - 1)
        def _():
            _pack_rows(y_ref, acc_ref[...])

    @pl.when((t >= nu_ref[0]) & (f == pl.num_programs(1) - 1))
    def _():
        y_ref[...] = jnp.zeros(y_ref.shape, y_ref.dtype)


def _moe_experts(plan, xb, wg, wu, wd):
    d, ff = wg.shape[1], wg.shape[2]
    tile = MOE_TILE
    n_tiles = xb.shape[0] // (tile * SUBLANES)
    tf = _tile(MOE_COLS, ff)
    nf = ff // tf

    def fcol(t, f, nu):
        return jnp.where(t < nu[0], f, nf - 1)

    spec = pltpu.PrefetchScalarGridSpec(
        num_scalar_prefetch=2,
        grid=(n_tiles, nf),
        in_specs=[pl.BlockSpec((tile * SUBLANES, LANES), lambda t, f, te, nu: (jnp.minimum(t, nu[0] - 1), 0)),
                  pl.BlockSpec((1, d, tf), lambda t, f, te, nu: (te[t], 0, fcol(t, f, nu))),
                  pl.BlockSpec((1, d, tf), lambda t, f, te, nu: (te[t], 0, fcol(t, f, nu))),
                  pl.BlockSpec((1, tf, d), lambda t, f, te, nu: (te[t], fcol(t, f, nu), 0))],
        out_specs=pl.BlockSpec((tile * SUBLANES, LANES), lambda t, f, te, nu: (t, 0)),
        scratch_shapes=[pltpu.VMEM((tile, d), F32)],
    )
    return pl.pallas_call(
        _expert_kernel,
        grid_spec=spec,
        out_shape=jax.ShapeDtypeStruct(xb.shape, U32),
        compiler_params=_params("arbitrary", "arbitrary"),
        name="moe_experts",
    )(plan["tile_e"], plan["n_used"], xb, wg, wu, wd)


def _combine_kernel(d1_ref, d2_ref, rc_ref, h_ref, g_ref, y_hbm, *rest, final):
    outs, (y1_ref, y2_ref, sem) = rest[:-3], rest[-3:]
    chunk = h_ref.shape[0]
    base = pl.program_id(0) * chunk

    def body(i, carry):
        t = base + i
        pltpu.make_async_copy(_row_tile(y_hbm, d1_ref[t]), _row_tile(y1_ref, i), sem).start()
        pltpu.make_async_copy(_row_tile(y_hbm, d2_ref[t]), _row_tile(y2_ref, i), sem).start()
        return carry

    lax.fori_loop(0, chunk, body, 0)
    _wait_rows(y_hbm, y1_ref, sem, chunk)
    _wait_rows(y_hbm, y2_ref, sem, chunk)
    rc = rc_ref[...]
    w1, w2 = rc[:, 4:5], rc[:, 5:6]
    lo1, hi1 = _unpack_rows(y1_ref, chunk)
    lo2, hi2 = _unpack_rows(y2_ref, chunk)
    y = jnp.concatenate([w1 * lo1 + w2 * lo2, w1 * hi1 + w2 * hi2], axis=1)
    _write_epilogue(h_ref[...] + y, g_ref[...], outs, final)


def _moe_combine(plan, rc, yb, h, g, *, final):
    n, d = h.shape
    chunk = _tile(MOE_CHUNK, n)
    tok = lambda c, d1, d2: (c, 0)
    shapes = _epilogue_shapes(n, d, final)
    spec = pltpu.PrefetchScalarGridSpec(
        num_scalar_prefetch=2,
        grid=(n // chunk,),
        in_specs=[pl.BlockSpec((chunk, LANES), tok),
                  pl.BlockSpec((chunk, d), tok),
                  pl.BlockSpec((1, d), lambda c, d1, d2: (0, 0)),
                  pl.BlockSpec(memory_space=pl.ANY)],
        out_specs=[pl.BlockSpec((chunk, d), tok) for _ in shapes],
        scratch_shapes=[pltpu.VMEM((chunk * SUBLANES, LANES), U32),
                        pltpu.VMEM((chunk * SUBLANES, LANES), U32),
                        pltpu.SemaphoreType.DMA(())],
    )
    return pl.pallas_call(
        functools.partial(_combine_kernel, final=final),
        grid_spec=spec,
        out_shape=shapes,
        compiler_params=_params("arbitrary"),
        name="moe_combine",
    )(plan["dest1"], plan["dest2"], rc, h, g, yb)


def _moe_ffn(a_packed, h, g_ffn, g_next, rw, rb, wg, wu, wd, *, final):
    n, d = h.shape
    n_exp = rw.shape[1]
    assert n_exp <= LANES and TOP_K == 2
    rw_pad = jnp.zeros((d, LANES), F32).at[:, :n_exp].set(rw)
    rb_pad = jnp.zeros((1, LANES), F32).at[0, :n_exp].set(rb)
    rc, cnt = _router(h, g_ffn, rw_pad, rb_pad, n_exp=n_exp)
    n_tiles = (n * TOP_K) // MOE_TILE + n_exp
    plan = _moe_plan(rc, cnt, n_exp=n_exp, n_tiles=n_tiles)
    xb = _moe_dispatch(plan, a_packed, n_rows=n_tiles * MOE_TILE)
    yb = _moe_experts(plan, xb, wg, wu, wd)
    return _moe_combine(plan, rc, yb, h, g_next, final=final)


def kernel(x, mix_norm, w_in, b_forget, conv_w, w_conv_out, w_attn_out, w_o, ffn_norm, dense_w_gate, dense_w_up, dense_w_down, router_w, router_b, moe_w_gate, moe_w_up, moe_w_down, final_norm):
    batch, seq, d = x.shape
    depth = mix_norm.shape[0]
    conv_k, d_conv = conv_w.shape[1], conv_w.shape[2]
    d_attn = w_attn_out.shape[1]
    n_heads = b_forget.shape[1]
    head_dim = d_attn // n_heads
    assert head_dim % LANES == 0 and conv_k - 1 <= SUBLANES
    n = batch * seq
    hp = 2 * SUBLANES
    assert n_heads <= hp
    scale = LOG2_E / math.sqrt(head_dim)
    o_q = 3 * d_conv
    o_f = o_q + 3 * d_attn
    o_g = o_f + n_heads

    h = x.reshape(n, d)
    a = _norm(h, mix_norm[0].reshape(1, d))
    out = None
    for layer in range(depth):
        wl = w_in[layer]
        w_bcu = wl[:, :o_q].astype(BF16)
        w_qkv = jnp.concatenate([wl[:, o_q:o_q + d_attn] * scale, wl[:, o_q + d_attn:o_f]], axis=1).astype(BF16)
        wf_t = jnp.zeros((hp, d), F32).at[:n_heads].set(wl[:, o_f:o_g].T).astype(BF16)
        bf_col = jnp.zeros((hp, 1), F32).at[:n_heads, 0].set(b_forget[layer])
        w_gates = wl[:, o_g:].astype(BF16)
        cw_pad = jnp.zeros((SUBLANES, d_conv), F32).at[:conv_k].set(conv_w[layer])

        z = _conv_front(a, w_bcu, cw_pad, d_conv=d_conv, seq=seq, conv_k=conv_k)
        qkv, c = _qkv_proj(a, w_qkv, wf_t, bf_col, batch=batch, seq=seq)
        gates = _gate_proj(a, w_gates)
        o = _attention(qkv, c, batch=batch, seq=seq, n_heads=n_heads, head_dim=head_dim)
        h, a = _mix_out(z, o, gates, h, w_conv_out[layer].astype(BF16), w_attn_out[layer].astype(BF16),
                        w_o[layer].astype(BF16), ffn_norm[layer].reshape(1, d), packed=layer % 2 == 1)

        final = layer == depth - 1
        g_next = (final_norm if final else mix_norm[layer + 1]).reshape(1, d)
        i = layer // 2
        if layer % 2 == 0:
            res = _dense_ffn(a, h, dense_w_gate[i].astype(BF16), dense_w_up[i].astype(BF16),
                             dense_w_down[i].astype(BF16), g_next, final=final)
        else:
            res = _moe_ffn(a, h, ffn_norm[layer].reshape(1, d), g_next, router_w[i], router_b[i],
                           moe_w_gate[i].astype(BF16), moe_w_up[i].astype(BF16),
                           moe_w_down[i].astype(BF16), final=final)
        if final:
            out = res[0]
        else:
            h, a = res
    return out.reshape(batch, seq, d)
```

```python
import functools
import math

import jax
import jax.numpy as jnp
from jax import lax
from jax.experimental import pallas as pl
from jax.experimental.pallas import tpu as pltpu

F32 = jnp.float32
BF16 = jnp.bfloat16
I32 = jnp.int32
U32 = jnp.uint32

RMS_EPS = 1e-6
LOG2_E = math.log2(math.e)
TOP_K = 2
LANES = 128
SUBLANES = 8
VMEM_LIMIT = 60 * 1024 * 1024

PROJ_ROWS = 1024
PROJ_COLS = 1024
GATE_COLS = 2048
CONV_COLS = 1024
ATTN_TILE = 512
ATTN_KEYS = 256
ATTN_UNROLL = 16
MIX_ROWS = 512
FFN_ROWS = 512
FFN_COLS = 512
MOE_CHUNK = 512
MOE_TILE = 1024
MOE_COLS = 1024


def _tile(pref, dim):
    t = min(pref, dim)
    assert dim % t == 0, (pref, dim)
    return t


def _params(*sem):
    return pltpu.CompilerParams(dimension_semantics=sem, vmem_limit_bytes=VMEM_LIMIT)


def _dot(a, b):
    return jnp.dot(a, b, preferred_element_type=F32)


def _dot_nt(a, b):
    return lax.dot_general(a, b, (((1,), (1,)), ((), ())), preferred_element_type=F32)


def _split3(x):
    hi = x.astype(BF16)
    r = x - hi.astype(F32)
    mid = r.astype(BF16)
    lo = (r - mid.astype(F32)).astype(BF16)
    return hi, mid, lo


def _dot_f32_by_01(x, onehot):
    hi, mid, lo = _split3(x)
    return _dot(hi, onehot) + _dot(mid, onehot) + _dot(lo, onehot)


def _dot_f32(a, b):
    a1, a2, a3 = _split3(a)
    b1, b2, b3 = _split3(b)
    small = _dot(a1, b3) + _dot(a3, b1) + _dot(a2, b2)
    mid = _dot(a1, b2) + _dot(a2, b1)
    return _dot(a1, b1) + (mid + small)


def _rms(xf, g):
    ms = jnp.mean(xf * xf, axis=-1, keepdims=True)
    return xf * lax.rsqrt(ms + RMS_EPS) * g


def _pack_rows(ref, x):
    m, d = x.shape
    half = d // 2
    assert half == SUBLANES * LANES
    lo = lax.bitcast_convert_type(x[:, :half].astype(BF16).astype(F32), U32) >> 16
    hi = lax.bitcast_convert_type(x[:, half:].astype(BF16).astype(F32), U32) & jnp.uint32(0xFFFF0000)
    w = hi | lo
    for r in range(SUBLANES):
        ref[pl.ds(r, m, stride=SUBLANES), :] = w[:, r * LANES:(r + 1) * LANES]


def _unpack_rows(ref, m):
    w = jnp.concatenate([ref[pl.ds(r, m, stride=SUBLANES), :] for r in range(SUBLANES)], axis=1)
    lo = lax.bitcast_convert_type(w << 16, F32)
    hi = lax.bitcast_convert_type(w & jnp.uint32(0xFFFF0000), F32)
    return lo, hi


def _norm_kernel(x_ref, g_ref, a_ref):
    a_ref[...] = _rms(x_ref[...], g_ref[...]).astype(a_ref.dtype)


def _norm(x, g):
    n, d = x.shape
    tm = _tile(FFN_ROWS, n)
    return pl.pallas_call(
        _norm_kernel,
        grid=(n // tm,),
        in_specs=[pl.BlockSpec((tm, d), lambda i: (i, 0)),
                  pl.BlockSpec((1, d), lambda i: (0, 0))],
        out_specs=pl.BlockSpec((tm, d), lambda i: (i, 0)),
        out_shape=jax.ShapeDtypeStruct((n, d), BF16),
        compiler_params=_params("parallel"),
        name="rmsnorm",
    )(x, g)


def _gate_kernel(a_ref, w_ref, o_ref):
    o_ref[...] = jax.nn.sigmoid(_dot(a_ref[...], w_ref[...])).astype(o_ref.dtype)


def _gate_proj(a, w):
    n, k = a.shape
    p = w.shape[1]
    tm = _tile(PROJ_ROWS, n)
    tn = _tile(GATE_COLS, p)
    return pl.pallas_call(
        _gate_kernel,
        grid=(n // tm, p // tn),
        in_specs=[pl.BlockSpec((tm, k), lambda i, j: (i, 0)),
                  pl.BlockSpec((k, tn), lambda i, j: (0, j))],
        out_specs=pl.BlockSpec((tm, tn), lambda i, j: (i, j)),
        out_shape=jax.ShapeDtypeStruct((n, p), BF16),
        compiler_params=_params("parallel", "arbitrary"),
        name="gate_proj",
    )(a, w)


def _conv_kernel(a_ref, wb_ref, wc_ref, wu_ref, cw_ref, z_ref, carry_ref, *, tiles_per_seq, conv_k):
    i = pl.program_id(0)
    j = pl.program_id(1)

    @pl.when(i % tiles_per_seq == 0)
    def _():
        carry_ref[j] = jnp.zeros(carry_ref.shape[1:], F32)

    a = a_ref[...]
    b = _dot(a, wb_ref[...])
    cu = _dot(a, wc_ref[...]) * _dot(a, wu_ref[...])
    tm = cu.shape[0]
    prev = carry_ref[j]
    row = lax.broadcasted_iota(I32, cu.shape, 0)
    cw = cw_ref[...]
    y = cu * cw[conv_k - 1:conv_k]
    for s in range(1, conv_k):
        sh = pltpu.roll(cu, s, 0)
        for r in range(s):
            src = SUBLANES - s + r
            sh = jnp.where(row == r, prev[src:src + 1], sh)
        y = y + sh * cw[conv_k - 1 - s:conv_k - s]
    z_ref[...] = (b * y).astype(z_ref.dtype)
    carry_ref[j] = cu[tm - SUBLANES:tm]


def _conv_front(a, w_bcu, conv_w_pad, *, d_conv, seq, conv_k):
    n, k = a.shape
    tm = _tile(PROJ_ROWS, seq)
    tn = _tile(CONV_COLS, d_conv)
    nj = d_conv // tn
    return pl.pallas_call(
        functools.partial(_conv_kernel, tiles_per_seq=seq // tm, conv_k=conv_k),
        grid=(n // tm, nj),
        in_specs=[pl.BlockSpec((tm, k), lambda i, j: (i, 0)),
                  pl.BlockSpec((k, tn), lambda i, j: (0, j)),
                  pl.BlockSpec((k, tn), lambda i, j: (0, nj + j)),
                  pl.BlockSpec((k, tn), lambda i, j: (0, 2 * nj + j)),
                  pl.BlockSpec((SUBLANES, tn), lambda i, j: (0, j))],
        out_specs=pl.BlockSpec((tm, tn), lambda i, j: (i, j)),
        out_shape=jax.ShapeDtypeStruct((n, d_conv), BF16),
        scratch_shapes=[pltpu.VMEM((nj, SUBLANES, tn), F32)],
        compiler_params=_params("arbitrary", "arbitrary"),
        name="conv_front",
    )(a, w_bcu, w_bcu, w_bcu, conv_w_pad)


def _qkv_kernel(a_ref, w_ref, wf_ref, bf_ref, o_ref, c_ref, carry_ref, *, tiles_per_seq):
    i = pl.program_id(0)
    j = pl.program_id(1)
    a = a_ref[...]
    o_ref[...] = _dot(a, w_ref[...]).astype(o_ref.dtype)

    @pl.when(j == 0)
    def _():
        @pl.when(i % tiles_per_seq == 0)
        def _():
            carry_ref[...] = jnp.zeros(carry_ref.shape, F32)

        lf = jax.nn.log_sigmoid(_dot_nt(wf_ref[...], a) + bf_ref[...])
        tm = lf.shape[1]
        r = lax.broadcasted_iota(I32, (tm, tm), 0)
        c = lax.broadcasted_iota(I32, (tm, tm), 1)
        upper = jnp.where(r <= c, 1.0, 0.0).astype(BF16)
        cs = _dot_f32_by_01(lf, upper) + carry_ref[:, 0:1]
        c_ref[0] = cs * LOG2_E
        carry_ref[...] = jnp.broadcast_to(cs[:, tm - 1:tm], carry_ref.shape)


def _qkv_proj(a, w, wf_t, bf_col, *, batch, seq):
    n, k = a.shape
    p = w.shape[1]
    hp = wf_t.shape[0]
    tm = _tile(PROJ_ROWS, seq)
    tn = _tile(PROJ_COLS, p)
    ns = seq // tm
    return pl.pallas_call(
        functools.partial(_qkv_kernel, tiles_per_seq=ns),
        grid=(n // tm, p // tn),
        in_specs=[pl.BlockSpec((tm, k), lambda i, j: (i, 0)),
                  pl.BlockSpec((k, tn), lambda i, j: (0, j)),
                  pl.BlockSpec((hp, k), lambda i, j: (0, 0)),
                  pl.BlockSpec((hp, 1), lambda i, j: (0, 0))],
        out_specs=[pl.BlockSpec((tm, tn), lambda i, j: (i, j)),
                   pl.BlockSpec((1, hp, tm), lambda i, j: (i // ns, 0, i % ns))],
        out_shape=[jax.ShapeDtypeStruct((n, p), BF16),
                   jax.ShapeDtypeStruct((batch, hp, seq), F32)],
        scratch_shapes=[pltpu.VMEM((hp, LANES), F32)],
        compiler_params=_params("arbitrary", "arbitrary"),
        name="qkv_proj",
    )(a, w, wf_t, bf_col)


def _attn_kernel(q_ref, k_ref, v_ref, c_ref, o_ref, s_ref, p_ref, al_ref, m_ref, acc_ref, *, t, tk):
    qi = pl.program_id(2)
    dh = q_ref.shape[1]
    nd = t // tk
    reps = tk // LANES
    q0 = pl.multiple_of(qi * t, t)
    c0 = c_ref[0, 0, :, pl.ds(q0, LANES)][:, 0:1]
    ones_col = jnp.where(lax.broadcasted_iota(I32, (tk, LANES), 1) == 0, 1.0, 0.0).astype(BF16)

    def key_start(item):
        full = jnp.maximum(item - nd, 0) * tk
        return pl.multiple_of(jnp.where(item < nd, q0 + jnp.maximum(item, 0) * tk, full), tk)

    def score(slot, item, diag):
        k0 = key_start(item)
        s = _dot_nt(q_ref[...], k_ref[pl.ds(k0, tk), :]) - (c_ref[0, 0, :, pl.ds(k0, tk)] - c0)
        if diag is not None:
            row = lax.broadcasted_iota(I32, (t, tk), 0)
            col = lax.broadcasted_iota(I32, (t, tk), 1) + diag * tk
            s = jnp.where(col <= row, s, -jnp.inf)
        s_ref[slot] = s

    def softmax(slot):
        s = s_ref[slot]
        m_prev = m_ref[...]
        m_new = jnp.maximum(m_prev, jnp.max(s, axis=-1, keepdims=True))
        al_ref[slot] = jnp.exp2(m_prev - m_new)
        p_ref[slot] = jnp.exp2(s - jnp.concatenate([m_new] * reps, axis=1)).astype(BF16)
        m_ref[...] = m_new

    def apply(slot, item):
        vaug = jnp.concatenate([v_ref[pl.ds(key_start(item), tk), :], ones_col], axis=1)
        al = jnp.concatenate([al_ref[slot]] * (acc_ref.shape[1] // LANES), axis=1)
        acc_ref[...] = al * acc_ref[...] + _dot(p_ref[slot], vaug)

    def step(i, slot, diag=None):
        score(1 - slot, i + 1, diag)
        softmax(slot)
        apply(1 - slot, i - 1)

    def drain(last, slot):
        softmax(slot)
        apply(1 - slot, last - 1)
        apply(slot, last)

    m_ref[...] = jnp.full(m_ref.shape, -jnp.inf, F32)
    acc_ref[...] = jnp.zeros(acc_ref.shape, F32)
    p_ref[1] = jnp.zeros(p_ref.shape[1:], BF16)
    al_ref[1] = jnp.ones(al_ref.shape[1:], F32)
    score(0, 0, 0)
    for i in range(nd - 1):
        step(i, i % 2, diag=i + 1)

    first = nd - 1
    n_full = qi * nd
    last = first + n_full

    def body(g, carry):
        for u in range(ATTN_UNROLL):
            step(first + g * ATTN_UNROLL + u, (first + u) % 2)
        return carry

    groups = n_full // ATTN_UNROLL
    lax.fori_loop(0, groups, body, 0)
    base = first + groups * ATTN_UNROLL
    for rem in range(ATTN_UNROLL):
        @pl.when(last - base == rem)
        def _(rem=rem):
            for u in range(rem):
                step(base + u, (first + u) % 2)
            drain(last, (first + rem) % 2)

    acc = acc_ref[...]
    o_ref[...] = (acc[:, :dh] / acc[:, dh:dh + 1]).astype(o_ref.dtype)


def _attention(qkv, c, *, batch, seq, n_heads, head_dim):
    n = qkv.shape[0]
    t = _tile(ATTN_TILE, seq)
    tk = _tile(ATTN_KEYS, t)
    assert ATTN_UNROLL % 2 == 0
    nq = seq // t
    c = c.reshape(batch, c.shape[1], 1, seq)
    return pl.pallas_call(
        functools.partial(_attn_kernel, t=t, tk=tk),
        grid=(batch, n_heads, nq),
        in_specs=[pl.BlockSpec((t, head_dim), lambda b, h, i: (b * nq + i, h)),
                  pl.BlockSpec((seq, head_dim), lambda b, h, i: (b, n_heads + h)),
                  pl.BlockSpec((seq, head_dim), lambda b, h, i: (b, 2 * n_heads + h)),
                  pl.BlockSpec((1, 1, 1, seq), lambda b, h, i: (b, h, 0, 0))],
        out_specs=pl.BlockSpec((t, head_dim), lambda b, h, i: (b * nq + i, h)),
        out_shape=jax.ShapeDtypeStruct((n, n_heads * head_dim), BF16),
        scratch_shapes=[pltpu.VMEM((2, t, tk), F32), pltpu.VMEM((2, t, tk), BF16),
                        pltpu.VMEM((2, t, LANES), F32), pltpu.VMEM((t, LANES), F32),
                        pltpu.VMEM((t, head_dim + LANES), F32)],
        compiler_params=_params("parallel", "parallel", "arbitrary"),
        name="forget_attn",
    )(qkv, qkv, qkv, c)


def _mix_kernel(z_ref, o_ref, ga_ref, gb_ref, h_ref, wc_ref, wa_ref, wo_ref, g_ref, hn_ref, a_ref, *, packed):
    yc = _dot(z_ref[...], wc_ref[...])
    ya = _dot(o_ref[...], wa_ref[...])
    m = ga_ref[...].astype(F32) * yc + gb_ref[...].astype(F32) * ya
    hn = h_ref[...] + _dot(m.astype(BF16), wo_ref[...])
    hn_ref[...] = hn
    a = _rms(hn, g_ref[...])
    if packed:
        _pack_rows(a_ref, a)
    else:
        a_ref[...] = a.astype(a_ref.dtype)


def _mix_out(z, o, gates, h, wc, wa, wo, g, *, packed):
    n, d = h.shape
    dc = z.shape[1]
    da = o.shape[1]
    tm = _tile(MIX_ROWS, n)
    const = lambda i: (0, 0)
    if packed:
        a_spec = pl.BlockSpec((tm * SUBLANES, LANES), lambda i: (i, 0))
        a_shape = jax.ShapeDtypeStruct((n * SUBLANES, LANES), U32)
    else:
        a_spec = pl.BlockSpec((tm, d), lambda i: (i, 0))
        a_shape = jax.ShapeDtypeStruct((n, d), BF16)
    return pl.pallas_call(
        functools.partial(_mix_kernel, packed=packed),
        grid=(n // tm,),
        in_specs=[pl.BlockSpec((tm, dc), lambda i: (i, 0)),
                  pl.BlockSpec((tm, da), lambda i: (i, 0)),
                  pl.BlockSpec((tm, d), lambda i: (i, 0)),
                  pl.BlockSpec((tm, d), lambda i: (i, 1)),
                  pl.BlockSpec((tm, d), lambda i: (i, 0)),
                  pl.BlockSpec((dc, d), const, pipeline_mode=pl.Buffered(1)),
                  pl.BlockSpec((da, d), const, pipeline_mode=pl.Buffered(1)),
                  pl.BlockSpec((d, d), const, pipeline_mode=pl.Buffered(1)),
                  pl.BlockSpec((1, d), const)],
        out_specs=[pl.BlockSpec((tm, d), lambda i: (i, 0)), a_spec],
        out_shape=[jax.ShapeDtypeStruct((n, d), F32), a_shape],
        compiler_params=_params("parallel"),
        name="mix_out",
    )(z, o, gates, gates, h, wc, wa, wo, g)


def _epilogue_shapes(n, d, final):
    if final:
        return [jax.ShapeDtypeStruct((n, d), F32)]
    return [jax.ShapeDtypeStruct((n, d), F32), jax.ShapeDtypeStruct((n, d), BF16)]


def _write_epilogue(hn, g, outs, final):
    if final:
        outs[0][...] = _rms(hn, g)
    else:
        outs[0][...] = hn
        outs[1][...] = _rms(hn, g).astype(outs[1].dtype)


def _ffn_kernel(a_ref, wg_ref, wu_ref, wd_ref, h_ref, g_ref, *rest, final):
    outs, acc_ref = rest[:-1], rest[-1]
    f = pl.program_id(1)

    @pl.when(f == 0)
    def _():
        acc_ref[...] = h_ref[...]

    a = a_ref[...]
    hid = (jax.nn.silu(_dot(a, wg_ref[...])) * _dot(a, wu_ref[...])).astype(BF16)
    acc_ref[...] += _dot(hid, wd_ref[...])

    @pl.when(f == pl.num_programs(1) - 1)
    def _():
        _write_epilogue(acc_ref[...], g_ref[...], outs, final)


def _dense_ffn(a, h, wg, wu, wd, g, *, final):
    n, d = h.shape
    ff = wg.shape[1]
    tm = _tile(FFN_ROWS, n)
    tf = _tile(FFN_COLS, ff)
    row = lambda i, f: (i, 0)
    shapes = _epilogue_shapes(n, d, final)
    return pl.pallas_call(
        functools.partial(_ffn_kernel, final=final),
        grid=(n // tm, ff // tf),
        in_specs=[pl.BlockSpec((tm, d), row),
                  pl.BlockSpec((d, tf), lambda i, f: (0, f)),
                  pl.BlockSpec((d, tf), lambda i, f: (0, f)),
                  pl.BlockSpec((tf, d), lambda i, f: (f, 0)),
                  pl.BlockSpec((tm, d), row),
                  pl.BlockSpec((1, d), lambda i, f: (0, 0))],
        out_specs=[pl.BlockSpec((tm, d), row) for _ in shapes],
        out_shape=shapes,
        scratch_shapes=[pltpu.VMEM((tm, d), F32)],
        compiler_params=_params("parallel", "arbitrary"),
        name="dense_ffn",
    )(a, wg, wu, wd, h, g)


def _router_kernel(h_ref, g_ref, rw_ref, rb_ref, rc_ref, cnt_ref, carry_ref, *, n_exp):
    i = pl.program_id(0)

    @pl.when(i == 0)
    def _():
        carry_ref[...] = jnp.zeros(carry_ref.shape, F32)

    a = _rms(h_ref[...], g_ref[...])
    tm = a.shape[0]
    lane = lax.broadcasted_iota(I32, (tm, LANES), 1)
    lanef = lane.astype(F32)
    logits = jnp.where(lane < n_exp, _dot_f32(a, rw_ref[...]) + rb_ref[...], -jnp.inf)
    big = float(LANES)
    m1 = jnp.max(logits, axis=-1, keepdims=True)
    i1 = jnp.min(jnp.where(logits == m1, lanef, big), axis=-1, keepdims=True)
    rest = jnp.where(lanef == i1, -jnp.inf, logits)
    m2 = jnp.max(rest, axis=-1, keepdims=True)
    i2 = jnp.min(jnp.where(rest == m2, lanef, big), axis=-1, keepdims=True)
    e2 = jnp.exp(m2 - m1)
    w1 = 1.0 / (1.0 + e2)
    w2 = e2 / (1.0 + e2)

    onehot = jnp.where(lanef == i1, 1.0, jnp.where(lanef == i2, 1.0, 0.0))
    r = lax.broadcasted_iota(I32, (tm, tm), 0)
    c = lax.broadcasted_iota(I32, (tm, tm), 1)
    strict_lower = jnp.where(c < r, 1.0, 0.0).astype(BF16)
    before = carry_ref[0:1, :]
    prefix = _dot(strict_lower, onehot.astype(BF16)) + before
    rank1 = jnp.sum(jnp.where(lanef == i1, prefix, 0.0), axis=-1, keepdims=True)
    rank2 = jnp.sum(jnp.where(lanef == i2, prefix, 0.0), axis=-1, keepdims=True)
    after = before + jnp.sum(onehot, axis=0, keepdims=True)

    srow = lax.broadcasted_iota(I32, (SUBLANES, LANES), 0)
    cnt_ref[0] = jnp.where(srow == 0, before, jnp.where(srow == 1, after, 0.0))
    carry_ref[...] = jnp.broadcast_to(after, carry_ref.shape)

    rec = jnp.where(lane == 0, i1, jnp.where(lane == 1, i2, jnp.where(
        lane == 2, rank1, jnp.where(lane == 3, rank2, jnp.where(
            lane == 4, w1, jnp.where(lane == 5, w2, 0.0))))))
    rc_ref[...] = rec


def _router(h, g, rw_pad, rb_pad, *, n_exp):
    n, d = h.shape
    tm = _tile(MOE_CHUNK, n)
    nc = n // tm
    return pl.pallas_call(
        functools.partial(_router_kernel, n_exp=n_exp),
        grid=(nc,),
        in_specs=[pl.BlockSpec((tm, d), lambda i: (i, 0)),
                  pl.BlockSpec((1, d), lambda i: (0, 0)),
                  pl.BlockSpec((d, LANES), lambda i: (0, 0)),
                  pl.BlockSpec((1, LANES), lambda i: (0, 0))],
        out_specs=[pl.BlockSpec((tm, LANES), lambda i: (i, 0)),
                   pl.BlockSpec((1, SUBLANES, LANES), lambda i: (i, 0, 0))],
        out_shape=[jax.ShapeDtypeStruct((n, LANES), F32),
                   jax.ShapeDtypeStruct((nc, SUBLANES, LANES), F32)],
        scratch_shapes=[pltpu.VMEM((SUBLANES, LANES), F32)],
        compiler_params=_params("arbitrary"),
        name="moe_router",
    )(h, g, rw_pad, rb_pad)


def _moe_plan(rc, cnt, *, n_exp, n_tiles):
    counts = cnt[-1, 1, :n_exp].astype(I32)
    tiles_e = (counts + MOE_TILE - 1) // MOE_TILE
    tile_end = jnp.cumsum(tiles_e)
    n_used = tile_end[-1]
    pad_start = (tile_end - tiles_e) * MOE_TILE
    t_ids = jnp.arange(n_tiles, dtype=I32)
    tile_e = jnp.sum(jnp.minimum(t_ids, n_used - 1)[:, None] >= tile_end[None, :], axis=1).astype(I32)
    rec = rc[:, :2 * TOP_K].astype(I32)
    start_of = lambda e: jnp.sum(jnp.where(e[:, None] == jnp.arange(n_exp, dtype=I32)[None, :], pad_start[None, :], 0), axis=1)
    dest1 = start_of(rec[:, 0]) + rec[:, 2]
    dest2 = start_of(rec[:, 1]) + rec[:, 3]
    return dict(tile_e=tile_e, n_used=n_used.reshape(1).astype(I32), dest1=dest1.astype(I32), dest2=dest2.astype(I32))


def _row_tile(ref, row):
    return ref.at[pl.ds(pl.multiple_of(row * SUBLANES, SUBLANES), SUBLANES), :]


def _wait_rows(src_ref, dst_ref, sem, n_rows):
    n = n_rows * SUBLANES
    pltpu.make_async_copy(src_ref.at[pl.ds(0, n), :], dst_ref.at[pl.ds(0, n), :], sem).wait()


def _dispatch_kernel(d1_ref, d2_ref, a_ref, zeros_hbm, xb_hbm, sem):
    del zeros_hbm
    chunk = a_ref.shape[0] // SUBLANES
    base = pl.program_id(0) * chunk

    def body(i, carry):
        t = base + i
        src = _row_tile(a_ref, i)
        pltpu.make_async_copy(src, _row_tile(xb_hbm, d1_ref[t]), sem).start()
        pltpu.make_async_copy(src, _row_tile(xb_hbm, d2_ref[t]), sem).start()
        return carry

    lax.fori_loop(0, chunk, body, 0)
    for _ in range(TOP_K):
        _wait_rows(a_ref, xb_hbm, sem, chunk)


def _moe_dispatch(plan, a_packed, *, n_rows):
    n = a_packed.shape[0] // SUBLANES
    chunk = _tile(MOE_CHUNK, n)
    any_spec = pl.BlockSpec(memory_space=pl.ANY)
    spec = pltpu.PrefetchScalarGridSpec(
        num_scalar_prefetch=2,
        grid=(n // chunk,),
        in_specs=[pl.BlockSpec((chunk * SUBLANES, LANES), lambda c, d1, d2: (c, 0)), any_spec],
        out_specs=any_spec,
        scratch_shapes=[pltpu.SemaphoreType.DMA(())],
    )
    zeros = jnp.zeros((n_rows * SUBLANES, LANES), U32)
    return pl.pallas_call(
        _dispatch_kernel,
        grid_spec=spec,
        out_shape=jax.ShapeDtypeStruct(zeros.shape, U32),
        input_output_aliases={3: 0},
        compiler_params=_params("arbitrary"),
        name="moe_dispatch",
    )(plan["dest1"], plan["dest2"], a_packed, zeros)


def _expert_kernel(te_ref, nu_ref, x_ref, wg_ref, wu_ref, wd_ref, y_ref, acc_ref):
    t = pl.program_id(0)
    f = pl.program_id(1)
    rows = acc_ref.shape[0]

    @pl.when(t < nu_ref[0])
    def _():
        @pl.when(f == 0)
        def _():
            acc_ref[...] = jnp.zeros(acc_ref.shape, F32)

        lo, hi = _unpack_rows(x_ref, rows)
        x = jnp.concatenate([lo.astype(BF16), hi.astype(BF16)], axis=1)
        hid = (jax.nn.silu(_dot(x, wg_ref[0])) * _dot(x, wu_ref[0])).astype(BF16)
        acc_ref[...] += _dot(hid, wd_ref[0])

        @pl.when(f == pl.num_programs(1) - 1)
        def _():
            _pack_rows(y_ref, acc_ref[...])

    @pl.when((t >= nu_ref[0]) & (f == pl.num_programs(1) - 1))
    def _():
        y_ref[...] = jnp.zeros(y_ref.shape, y_ref.dtype)


def _moe_experts(plan, xb, wg, wu, wd):
    d, ff = wg.shape[1], wg.shape[2]
    tile = MOE_TILE
    n_tiles = xb.shape[0] // (tile * SUBLANES)
    tf = _tile(MOE_COLS, ff)
    nf = ff // tf

    def fcol(t, f, nu):
        return jnp.where(t < nu[0], f, nf - 1)

    spec = pltpu.PrefetchScalarGridSpec(
        num_scalar_prefetch=2,
        grid=(n_tiles, nf),
        in_specs=[pl.BlockSpec((tile * SUBLANES, LANES), lambda t, f, te, nu: (jnp.minimum(t, nu[0] - 1), 0)),
                  pl.BlockSpec((1, d, tf), lambda t, f, te, nu: (te[t], 0, fcol(t, f, nu))),
                  pl.BlockSpec((1, d, tf), lambda t, f, te, nu: (te[t], 0, fcol(t, f, nu))),
                  pl.BlockSpec((1, tf, d), lambda t, f, te, nu: (te[t], fcol(t, f, nu), 0))],
        out_specs=pl.BlockSpec((tile * SUBLANES, LANES), lambda t, f, te, nu: (t, 0)),
        scratch_shapes=[pltpu.VMEM((tile, d), F32)],
    )
    return pl.pallas_call(
        _expert_kernel,
        grid_spec=spec,
        out_shape=jax.ShapeDtypeStruct(xb.shape, U32),
        compiler_params=_params("arbitrary", "arbitrary"),
        name="moe_experts",
    )(plan["tile_e"], plan["n_used"], xb, wg, wu, wd)


def _combine_kernel(d1_ref, d2_ref, rc_ref, h_ref, g_ref, y_hbm, *rest, final):
    outs, (y1_ref, y2_ref, sem) = rest[:-3], rest[-3:]
    chunk = h_ref.shape[0]
    base = pl.program_id(0) * chunk

    def body(i, carry):
        t = base + i
        pltpu.make_async_copy(_row_tile(y_hbm, d1_ref[t]), _row_tile(y1_ref, i), sem).start()
        pltpu.make_async_copy(_row_tile(y_hbm, d2_ref[t]), _row_tile(y2_ref, i), sem).start()
        return carry

    lax.fori_loop(0, chunk, body, 0)
    _wait_rows(y_hbm, y1_ref, sem, chunk)
    _wait_rows(y_hbm, y2_ref, sem, chunk)
    rc = rc_ref[...]
    w1, w2 = rc[:, 4:5], rc[:, 5:6]
    lo1, hi1 = _unpack_rows(y1_ref, chunk)
    lo2, hi2 = _unpack_rows(y2_ref, chunk)
    y = jnp.concatenate([w1 * lo1 + w2 * lo2, w1 * hi1 + w2 * hi2], axis=1)
    _write_epilogue(h_ref[...] + y, g_ref[...], outs, final)


def _moe_combine(plan, rc, yb, h, g, *, final):
    n, d = h.shape
    chunk = _tile(MOE_CHUNK, n)
    tok = lambda c, d1, d2: (c, 0)
    shapes = _epilogue_shapes(n, d, final)
    spec = pltpu.PrefetchScalarGridSpec(
        num_scalar_prefetch=2,
        grid=(n // chunk,),
        in_specs=[pl.BlockSpec((chunk, LANES), tok),
                  pl.BlockSpec((chunk, d), tok),
                  pl.BlockSpec((1, d), lambda c, d1, d2: (0, 0)),
                  pl.BlockSpec(memory_space=pl.ANY)],
        out_specs=[pl.BlockSpec((chunk, d), tok) for _ in shapes],
        scratch_shapes=[pltpu.VMEM((chunk * SUBLANES, LANES), U32),
                        pltpu.VMEM((chunk * SUBLANES, LANES), U32),
                        pltpu.SemaphoreType.DMA(())],
    )
    return pl.pallas_call(
        functools.partial(_combine_kernel, final=final),
        grid_spec=spec,
        out_shape=shapes,
        compiler_params=_params("arbitrary"),
        name="moe_combine",
    )(plan["dest1"], plan["dest2"], rc, h, g, yb)


def _moe_ffn(a_packed, h, g_ffn, g_next, rw, rb, wg, wu, wd, *, final):
    n, d = h.shape
    n_exp = rw.shape[1]
    assert n_exp <= LANES and TOP_K == 2
    rw_pad = jnp.zeros((d, LANES), F32).at[:, :n_exp].set(rw)
    rb_pad = jnp.zeros((1, LANES), F32).at[0, :n_exp].set(rb)
    rc, cnt = _router(h, g_ffn, rw_pad, rb_pad, n_exp=n_exp)
    n_tiles = (n * TOP_K) // MOE_TILE + n_exp
    plan = _moe_plan(rc, cnt, n_exp=n_exp, n_tiles=n_tiles)
    xb = _moe_dispatch(plan, a_packed, n_rows=n_tiles * MOE_TILE)
    yb = _moe_experts(plan, xb, wg, wu, wd)
    return _moe_combine(plan, rc, yb, h, g_next, final=final)


def kernel(x, mix_norm, w_in, b_forget, conv_w, w_conv_out, w_attn_out, w_o, ffn_norm, dense_w_gate, dense_w_up, dense_w_down, router_w, router_b, moe_w_gate, moe_w_up, moe_w_down, final_norm):
    batch, seq, d = x.shape
    depth = mix_norm.shape[0]
    conv_k, d_conv = conv_w.shape[1], conv_w.shape[2]
    d_attn = w_attn_out.shape[1]
    n_heads = b_forget.shape[1]
    head_dim = d_attn // n_heads
    assert head_dim % LANES == 0 and conv_k - 1 <= SUBLANES
    n = batch * seq
    hp = 2 * SUBLANES
    assert n_heads <= hp
    scale = LOG2_E / math.sqrt(head_dim)
    o_q = 3 * d_conv
    o_f = o_q + 3 * d_attn
    o_g = o_f + n_heads

    h = x.reshape(n, d)
    a = _norm(h, mix_norm[0].reshape(1, d))
    out = None
    for layer in range(depth):
        wl = w_in[layer]
        w_bcu = wl[:, :o_q].astype(BF16)
        w_qkv = jnp.concatenate([wl[:, o_q:o_q + d_attn] * scale, wl[:, o_q + d_attn:o_f]], axis=1).astype(BF16)
        wf_t = jnp.zeros((hp, d), F32).at[:n_heads].set(wl[:, o_f:o_g].T).astype(BF16)
        bf_col = jnp.zeros((hp, 1), F32).at[:n_heads, 0].set(b_forget[layer])
        w_gates = wl[:, o_g:].astype(BF16)
        cw_pad = jnp.zeros((SUBLANES, d_conv), F32).at[:conv_k].set(conv_w[layer])

        z = _conv_front(a, w_bcu, cw_pad, d_conv=d_conv, seq=seq, conv_k=conv_k)
        qkv, c = _qkv_proj(a, w_qkv, wf_t, bf_col, batch=batch, seq=seq)
        gates = _gate_proj(a, w_gates)
        o = _attention(qkv, c, batch=batch, seq=seq, n_heads=n_heads, head_dim=head_dim)
        h, a = _mix_out(z, o, gates, h, w_conv_out[layer].astype(BF16), w_attn_out[layer].astype(BF16),
                        w_o[layer].astype(BF16), ffn_norm[layer].reshape(1, d), packed=layer % 2 == 1)

        final = layer == depth - 1
        g_next = (final_norm if final else mix_norm[layer + 1]).reshape(1, d)
        i = layer // 2
        if layer % 2 == 0:
            res = _dense_ffn(a, h, dense_w_gate[i].astype(BF16), dense_w_up[i].astype(BF16),
                             dense_w_down[i].astype(BF16), g_next, final=final)
        else:
            res = _moe_ffn(a, h, ffn_norm[layer].reshape(1, d), g_next, router_w[i], router_b[i],
                           moe_w_gate[i].astype(BF16), moe_w_up[i].astype(BF16),
                           moe_w_down[i].astype(BF16), final=final)
        if final:
            out = res[0]
        else:
            h, a = res
    return out.reshape(batch, seq, d)
```
